```python
import jax, jax.numpy as jnp
from jax import lax
import numpy as np

D_MODEL = 1024
BATCH = 2
SEQ = 8192
DEPTH = 2

N_MIXERS = 2
CONV_WIDTH = 31
POOL_WINDOWS = (2, 4, 8, 16)
N_POOL_GROUPS = len(POOL_WINDOWS)
POOL_GROUP_DIM = D_MODEL // N_POOL_GROUPS
D_FF = ((8 * D_MODEL // 3 + 255) // 256) * 256
N_ADA = 6
EPS = 1e-6
N_CONV_LAYERS = (DEPTH + 1) // 2
N_POOL_LAYERS = DEPTH // 2

kernel_name = "hybrid_conformer_conv_multiscale_pool_trunk"


def rms_norm(x, g):
    xf = x.astype(jnp.float32)
    y = xf * lax.rsqrt(jnp.mean(xf * xf, axis=-1, keepdims=True) + EPS)
    return (y * g.astype(jnp.float32)).astype(x.dtype)


def layer_norm(x, g, b):
    xf = x.astype(jnp.float32)
    mu = jnp.mean(xf, axis=-1, keepdims=True)
    var = jnp.mean(jnp.square(xf - mu), axis=-1, keepdims=True)
    y = (xf - mu) * lax.rsqrt(var + EPS)
    return (y * g.astype(jnp.float32) + b.astype(jnp.float32)).astype(x.dtype)


def modulate(h, shift, scale):
    return h * (1.0 + scale[:, None, :]) + shift[:, None, :]


def conformer_conv(h, w1, b1, w_dw, b_dw, ln_g, ln_b, w2, b2):
    u = h @ w1 + b1
    a, g = jnp.split(u, 2, axis=-1)
    u = a * jax.nn.sigmoid(g)
    u = lax.conv_general_dilated(
        u, w_dw[:, None, :].astype(u.dtype),
        window_strides=(1,), padding=[(CONV_WIDTH - 1, 0)],
        dimension_numbers=("NWC", "WIO", "NWC"),
        feature_group_count=D_MODEL) + b_dw
    u = jax.nn.silu(layer_norm(u, ln_g, ln_b))
    return u @ w2 + b2


def multiscale_pool(h, w_grp, ls):
    B, S, D = h.shape
    hf = h.astype(jnp.float32).reshape(B, S, N_POOL_GROUPS, POOL_GROUP_DIM)
    cs = jnp.cumsum(hf, axis=1)
    cs = jnp.concatenate([jnp.zeros_like(cs[:, :1]), cs], axis=1)
    t = jnp.arange(S)
    pooled = []
    for gi, w in enumerate(POOL_WINDOWS):
        lo = jnp.maximum(t + 1 - w, 0)
        win_sum = cs[:, t + 1, gi] - cs[:, lo, gi]
        cnt = (t + 1 - lo).astype(jnp.float32)
        pooled.append(win_sum / cnt[None, :, None])
    pooled = jnp.stack(pooled, axis=2)
    mixed = (pooled - hf).astype(h.dtype)
    y = jnp.einsum("bsgc,gcd->bsgd", mixed, w_grp).reshape(B, S, D)
    return y * ls


def swiglu_ffn(h, w_gate, w_up, w_down):
    return (jax.nn.silu(h @ w_gate) * (h @ w_up)) @ w_down


def setup_inputs(seed: int = 0) -> dict:
    key = jax.random.key(seed)
    ks = iter(jax.random.split(key, 32))
    D, F = D_MODEL, D_FF
    nrm = lambda shape, s: jax.random.normal(next(ks), shape, jnp.float32) * s
    return {
        "x": nrm((BATCH, SEQ, D), 1.0),
        "c": nrm((BATCH, D), 1.0),
        "ada_w": nrm((DEPTH, D, N_ADA * D), 0.5 * D ** -0.5),
        "ada_b": nrm((DEPTH, N_ADA * D), 0.01),
        "norm_mix_g": 1.0 + nrm((DEPTH, D), 0.02),
        "norm_ffn_g": 1.0 + nrm((DEPTH, D), 0.02),
        "conv_w1": nrm((N_CONV_LAYERS, D, 2 * D), D ** -0.5),
        "conv_b1": nrm((N_CONV_LAYERS, 2 * D), 0.01),
        "conv_wdw": nrm((N_CONV_LAYERS, CONV_WIDTH, D), CONV_WIDTH ** -0.5),
        "conv_bdw": nrm((N_CONV_LAYERS, D), 0.01),
        "conv_ln_g": 1.0 + nrm((N_CONV_LAYERS, D), 0.02),
        "conv_ln_b": nrm((N_CONV_LAYERS, D), 0.01),
        "conv_w2": nrm((N_CONV_LAYERS, D, D), D ** -0.5),
        "conv_b2": nrm((N_CONV_LAYERS, D), 0.01),
        "pool_w": nrm((N_POOL_LAYERS, N_POOL_GROUPS, POOL_GROUP_DIM, POOL_GROUP_DIM), POOL_GROUP_DIM ** -0.5),
        "pool_ls": 1.0 + nrm((N_POOL_LAYERS, D), 0.02),
        "ffn_w_gate": nrm((DEPTH, D, F), D ** -0.5),
        "ffn_w_up": nrm((DEPTH, D, F), D ** -0.5),
        "ffn_w_down": nrm((DEPTH, F, D), F ** -0.5),
        "final_g": 1.0 + nrm((D,), 0.02),
    }


def reference(x, c, ada_w, ada_b, norm_mix_g, norm_ffn_g,
              conv_w1, conv_b1, conv_wdw, conv_bdw, conv_ln_g, conv_ln_b,
              conv_w2, conv_b2, pool_w, pool_ls,
              ffn_w_gate, ffn_w_up, ffn_w_down, final_g):
    c_act = jax.nn.silu(c)
    for i in range(DEPTH):
        mod = c_act @ ada_w[i] + ada_b[i]
        sh_m, sc_m, g_m, sh_f, sc_f, g_f = jnp.split(mod, N_ADA, axis=-1)
        h = modulate(rms_norm(x, norm_mix_g[i]), sh_m, sc_m)
        j = i // N_MIXERS
        if i % N_MIXERS == 0:
            y = conformer_conv(h, conv_w1[j], conv_b1[j], conv_wdw[j], conv_bdw[j],
                               conv_ln_g[j], conv_ln_b[j], conv_w2[j], conv_b2[j])
        else:
            y = multiscale_pool(h, pool_w[j], pool_ls[j])
        x = x + (1.0 + g_m)[:, None, :] * y
        h = modulate(rms_norm(x, norm_ffn_g[i]), sh_f, sc_f)
        y = swiglu_ffn(h, ffn_w_gate[i], ffn_w_up[i], ffn_w_down[i])
        x = x + (1.0 + g_f)[:, None, :] * y
    return rms_norm(x, final_g)
```

```python
import functools

import jax
import jax.numpy as jnp
from jax import lax
from jax.experimental import pallas as pl
from jax.experimental.pallas import tpu as pltpu

EPS = 1e-6
N_ADA = 6
CONV_WIDTH = 31
POOL_WINDOWS = (2, 4, 8, 16)

LANES = 128
SUBLANES = 8
CONV_HALO = 32
POOL_HALO = 16
CONV_ROWS = 64
SEQ_TILE = 512
ADA_TILE = 1536
VMEM_LIMIT = 56 * 1024 * 1024

_ARB2 = ("arbitrary", "arbitrary")


def _const_spec(shape):
    zeros = (0,) * len(shape)
    return pl.BlockSpec(shape, lambda b, s: zeros, pipeline_mode=pl.Buffered(1))


def _rms_mod(x, g, shift, scale):
    ms = jnp.mean(x * x, axis=-1, keepdims=True)
    y = x * lax.rsqrt(ms + EPS) * g
    return y * (1.0 + scale) + shift


def _silu(x):
    return x * jax.nn.sigmoid(x)


def _ada_kernel(c_ref, w_ref, b_ref, o_ref):
    ca = _silu(c_ref[...])
    o_ref[0] = jnp.dot(ca, w_ref[0], preferred_element_type=jnp.float32) + b_ref[0]


def _ada_mod(c, ada_w, ada_b):
    depth, d, n = ada_w.shape
    bsz = c.shape[0]
    rows = -(-bsz // SUBLANES) * SUBLANES
    c_pad = jnp.zeros((rows, d), c.dtype).at[:bsz].set(c)
    out = pl.pallas_call(
        _ada_kernel,
        grid=(depth, n // ADA_TILE),
        in_specs=[
            pl.BlockSpec((rows, d), lambda i, j: (0, 0)),
            pl.BlockSpec((1, d, ADA_TILE), lambda i, j: (i, 0, j)),
            pl.BlockSpec((1, 1, ADA_TILE), lambda i, j: (i, 0, j)),
        ],
        out_specs=pl.BlockSpec((1, rows, ADA_TILE), lambda i, j: (i, 0, j)),
        out_shape=jax.ShapeDtypeStruct((depth, rows, n), jnp.float32),
        compiler_params=pltpu.CompilerParams(
            dimension_semantics=_ARB2, vmem_limit_bytes=VMEM_LIMIT),
        name="ada_mod",
    )(c_pad, ada_w, ada_b.reshape(depth, 1, n))
    return out[:, :bsz].reshape(depth, bsz, N_ADA, d)


def _conv_mixer_kernel(x_ref, mod_ref, g_ref, w1_ref, b1_ref, wdw_ref, bdw_ref,
                       lng_ref, lnb_ref, w2_ref, b2_ref, o_ref, ubuf, ybuf):
    tm, d = x_ref.shape[1], x_ref.shape[2]

    @pl.when(pl.program_id(1) == 0)
    def _():
        ubuf[0:CONV_HALO, :] = jnp.zeros((CONV_HALO, d), jnp.float32)

    x = x_ref[0]
    m = mod_ref[0]
    h = _rms_mod(x, g_ref[...], m[0:1], m[1:2])
    u = jnp.dot(h.astype(jnp.bfloat16), w1_ref[...],
                preferred_element_type=jnp.float32) + b1_ref[...]
    ubuf[CONV_HALO:CONV_HALO + tm, :] = u[:, :d] * jax.nn.sigmoid(u[:, d:])

    for j in range(d // LANES):
        lanes = slice(j * LANES, (j + 1) * LANES)
        w = wdw_ref[:, lanes]
        wb = [jnp.broadcast_to(w[k:k + 1, :], (CONV_ROWS, LANES)) for k in range(CONV_WIDTH)]
        bias = bdw_ref[:, lanes]
        for i in range(tm // CONV_ROWS):
            acc = None
            for k in range(CONV_WIDTH):
                start = CONV_HALO + i * CONV_ROWS - (CONV_WIDTH - 1) + k
                term = ubuf[start:start + CONV_ROWS, lanes] * wb[k]
                acc = term if acc is None else acc + term
            ybuf[i * CONV_ROWS:(i + 1) * CONV_ROWS, lanes] = acc + bias

    ubuf[0:CONV_HALO, :] = ubuf[tm:tm + CONV_HALO, :]

    v = ybuf[...]
    mu = jnp.mean(v, axis=-1, keepdims=True)
    vc = v - mu
    var = jnp.mean(vc * vc, axis=-1, keepdims=True)
    v = _silu(vc * lax.rsqrt(var + EPS) * lng_ref[...] + lnb_ref[...])
    y = jnp.dot(v.astype(jnp.bfloat16), w2_ref[...],
                preferred_element_type=jnp.float32) + b2_ref[...]
    o_ref[0] = x + (1.0 + m[2:3]) * y


def _conv_mixer(x, mod, g, w1, b1, wdw, bdw, lng, lnb, w2, b2):
    bsz, seq, d = x.shape
    tm = SEQ_TILE
    row = lambda a: a.reshape(1, -1)
    return pl.pallas_call(
        _conv_mixer_kernel,
        grid=(bsz, seq // tm),
        in_specs=[
            pl.BlockSpec((1, tm, d), lambda b, s: (b, s, 0)),
            pl.BlockSpec((1, N_ADA, d), lambda b, s: (b, 0, 0)),
            _const_spec((1, d)),
            _const_spec((d, 2 * d)),
            _const_spec((1, 2 * d)),
            _const_spec((CONV_WIDTH, d)),
            _const_spec((1, d)),
            _const_spec((1, d)),
            _const_spec((1, d)),
            _const_spec((d, d)),
            _const_spec((1, d)),
        ],
        out_specs=pl.BlockSpec((1, tm, d), lambda b, s: (b, s, 0)),
        out_shape=jax.ShapeDtypeStruct(x.shape, x.dtype),
        scratch_shapes=[
            pltpu.VMEM((CONV_HALO + tm, d), jnp.float32),
            pltpu.VMEM((tm, d), jnp.float32),
        ],
        compiler_params=pltpu.CompilerParams(
            dimension_semantics=_ARB2, vmem_limit_bytes=VMEM_LIMIT),
        name="conv_mixer",
    )(x, mod, row(g), w1.astype(jnp.bfloat16), row(b1), wdw, row(bdw),
      row(lng), row(lnb), w2.astype(jnp.bfloat16), row(b2))


def _pool_mixer_kernel(x_ref, mod_ref, g_ref, pw_ref, ls_ref, o_ref, hbuf):
    tm, d = x_ref.shape[1], x_ref.shape[2]
    dg = d // len(POOL_WINDOWS)
    s = pl.program_id(1)

    @pl.when(s == 0)
    def _():
        hbuf[0:POOL_HALO, :] = jnp.zeros((POOL_HALO, d), jnp.float32)

    x = x_ref[0]
    m = mod_ref[0]
    h = _rms_mod(x, g_ref[...], m[0:1], m[1:2])
    hbuf[POOL_HALO:POOL_HALO + tm, :] = h

    t = s * tm + lax.broadcasted_iota(jnp.int32, (tm, dg), 0)
    for gi, win in enumerate(POOL_WINDOWS):
        lanes = slice(gi * dg, (gi + 1) * dg)
        hg = h[:, lanes]
        wsum = hg
        for j in range(1, win):
            wsum = wsum + hbuf[POOL_HALO - j:POOL_HALO - j + tm, lanes]
        cnt = jnp.minimum(t + 1, win).astype(jnp.float32)
        mixed = wsum / cnt - hg
        y = jnp.dot(mixed.astype(jnp.bfloat16), pw_ref[gi],
                    preferred_element_type=jnp.float32) * ls_ref[:, lanes]
        o_ref[0, :, lanes] = x[:, lanes] + (1.0 + m[2:3, lanes]) * y

    hbuf[0:POOL_HALO, :] = hbuf[tm:tm + POOL_HALO, :]


def _pool_mixer(x, mod, g, pw, ls):
    bsz, seq, d = x.shape
    tm = SEQ_TILE
    return pl.pallas_call(
        _pool_mixer_kernel,
        grid=(bsz, seq // tm),
        in_specs=[
            pl.BlockSpec((1, tm, d), lambda b, s: (b, s, 0)),
            pl.BlockSpec((1, N_ADA, d), lambda b, s: (b, 0, 0)),
            _const_spec((1, d)),
            _const_spec(pw.shape),
            _const_spec((1, d)),
        ],
        out_specs=pl.BlockSpec((1, tm, d), lambda b, s: (b, s, 0)),
        out_shape=jax.ShapeDtypeStruct(x.shape, x.dtype),
        scratch_shapes=[pltpu.VMEM((POOL_HALO + tm, d), jnp.float32)],
        compiler_params=pltpu.CompilerParams(
            dimension_semantics=_ARB2, vmem_limit_bytes=VMEM_LIMIT),
        name="pool_mixer",
    )(x, mod, g.reshape(1, -1), pw.astype(jnp.bfloat16), ls.reshape(1, -1))


def _ffn_kernel(x_ref, mod_ref, g_ref, wg_ref, wu_ref, wd_ref, fg_ref, o_ref, *, final):
    x = x_ref[0]
    m = mod_ref[0]
    h = _rms_mod(x, g_ref[...], m[3:4], m[4:5]).astype(jnp.bfloat16)
    a = jnp.dot(h, wg_ref[...], preferred_element_type=jnp.float32)
    b = jnp.dot(h, wu_ref[...], preferred_element_type=jnp.float32)
    act = (_silu(a) * b).astype(jnp.bfloat16)
    y = jnp.dot(act, wd_ref[...], preferred_element_type=jnp.float32)
    out = x + (1.0 + m[5:6]) * y
    if final:
        ms = jnp.mean(out * out, axis=-1, keepdims=True)
        out = out * lax.rsqrt(ms + EPS) * fg_ref[...]
    o_ref[0] = out


def _ffn(x, mod, g, wg, wu, wd, fg, final):
    bsz, seq, d = x.shape
    f = wg.shape[1]
    tm = SEQ_TILE
    return pl.pallas_call(
        functools.partial(_ffn_kernel, final=final),
        grid=(bsz, seq // tm),
        in_specs=[
            pl.BlockSpec((1, tm, d), lambda b, s: (b, s, 0)),
            pl.BlockSpec((1, N_ADA, d), lambda b, s: (b, 0, 0)),
            _const_spec((1, d)),
            _const_spec((d, f)),
            _const_spec((d, f)),
            _const_spec((f, d)),
            _const_spec((1, d)),
        ],
        out_specs=pl.BlockSpec((1, tm, d), lambda b, s: (b, s, 0)),
        out_shape=jax.ShapeDtypeStruct(x.shape, x.dtype),
        compiler_params=pltpu.CompilerParams(
            dimension_semantics=_ARB2, vmem_limit_bytes=VMEM_LIMIT),
        name="ffn_final" if final else "ffn",
    )(x, mod, g.reshape(1, -1), wg.astype(jnp.bfloat16), wu.astype(jnp.bfloat16),
      wd.astype(jnp.bfloat16), fg.reshape(1, -1))


def kernel(x, c, ada_w, ada_b, norm_mix_g, norm_ffn_g, conv_w1, conv_b1, conv_wdw, conv_bdw, conv_ln_g, conv_ln_b, conv_w2, conv_b2, pool_w, pool_ls, ffn_w_gate, ffn_w_up, ffn_w_down, final_g):
    depth = ada_w.shape[0]
    mod = _ada_mod(c, ada_w, ada_b)
    for i in range(depth):
        j = i // 2
        if i % 2 == 0:
            x = _conv_mixer(x, mod[i], norm_mix_g[i], conv_w1[j], conv_b1[j], conv_wdw[j],
                            conv_bdw[j], conv_ln_g[j], conv_ln_b[j], conv_w2[j], conv_b2[j])
        else:
            x = _pool_mixer(x, mod[i], norm_mix_g[i], pool_w[j], pool_ls[j])
        x = _ffn(x, mod[i], norm_ffn_g[i], ffn_w_gate[i], ffn_w_up[i], ffn_w_down[i],
                 final_g, final=(i == depth - 1))
    return x
```

```python
import functools

import jax
import jax.numpy as jnp
from jax import lax
from jax.experimental import pallas as pl
from jax.experimental.pallas import tpu as pltpu

EPS = 1e-6
N_ADA = 6
CONV_WIDTH = 31
POOL_WINDOWS = (2, 4, 8, 16)

LANES = 128
SUBLANES = 8
CONV_HALO = 32
POOL_HALO = 16
CONV_ROWS = 64
SEQ_TILE = 512
ADA_TILE = 1536
VMEM_LIMIT = 56 * 1024 * 1024

_ARB2 = ("arbitrary", "arbitrary")


def _const_spec(shape):
    zeros = (0,) * len(shape)
    return pl.BlockSpec(shape, lambda b, s: zeros, pipeline_mode=pl.Buffered(1))


def _rms_mod(x, g, shift, scale):
    ms = jnp.mean(x * x, axis=-1, keepdims=True)
    y = x * lax.rsqrt(ms + EPS) * g
    return y * (1.0 + scale) + shift


def _silu(x):
    return x * jax.nn.sigmoid(x)


ROW_PITCH = 2


def _shift_buffer(rows, d):
    return pltpu.VMEM((d // LANES, ROW_PITCH * rows, LANES), jnp.float32)


def _shift_rows(row0, n):
    return pl.ds(ROW_PITCH * row0, n, stride=ROW_PITCH)


def _shift_store(buf, row0, val):
    for j in range(val.shape[1] // LANES):
        buf[j, _shift_rows(row0, val.shape[0]), :] = val[:, j * LANES:(j + 1) * LANES]


def _shift_zero_halo(buf, halo):
    for j in range(buf.shape[0]):
        buf[j, _shift_rows(0, halo), :] = jnp.zeros((halo, LANES), jnp.float32)


def _shift_carry_halo(buf, halo, tm):
    for j in range(buf.shape[0]):
        buf[j, _shift_rows(0, halo), :] = buf[j, _shift_rows(tm, halo), :]


def _ada_kernel(c_ref, w_ref, b_ref, o_ref):
    ca = _silu(c_ref[...])
    o_ref[0] = jnp.dot(ca, w_ref[0], preferred_element_type=jnp.float32) + b_ref[0]


def _ada_mod(c, ada_w, ada_b):
    depth, d, n = ada_w.shape
    bsz = c.shape[0]
    rows = -(-bsz // SUBLANES) * SUBLANES
    c_pad = jnp.zeros((rows, d), c.dtype).at[:bsz].set(c)
    out = pl.pallas_call(
        _ada_kernel,
        grid=(depth, n // ADA_TILE),
        in_specs=[
            pl.BlockSpec((rows, d), lambda i, j: (0, 0)),
            pl.BlockSpec((1, d, ADA_TILE), lambda i, j: (i, 0, j)),
            pl.BlockSpec((1, 1, ADA_TILE), lambda i, j: (i, 0, j)),
        ],
        out_specs=pl.BlockSpec((1, rows, ADA_TILE), lambda i, j: (i, 0, j)),
        out_shape=jax.ShapeDtypeStruct((depth, rows, n), jnp.float32),
        compiler_params=pltpu.CompilerParams(
            dimension_semantics=_ARB2, vmem_limit_bytes=VMEM_LIMIT),
        name="ada_mod",
    )(c_pad, ada_w, ada_b.reshape(depth, 1, n))
    return out[:, :bsz].reshape(depth, bsz, N_ADA, d)


def _conv_mixer_kernel(x_ref, mod_ref, g_ref, w1_ref, b1_ref, wdw_ref, bdw_ref,
                       lng_ref, lnb_ref, w2_ref, b2_ref, o_ref, ubuf, ybuf):
    tm, d = x_ref.shape[1], x_ref.shape[2]

    @pl.when(pl.program_id(1) == 0)
    def _():
        _shift_zero_halo(ubuf, CONV_HALO)

    x = x_ref[0]
    m = mod_ref[0]
    h = _rms_mod(x, g_ref[...], m[0:1], m[1:2])
    u = jnp.dot(h.astype(jnp.bfloat16), w1_ref[...],
                preferred_element_type=jnp.float32) + b1_ref[...]
    _shift_store(ubuf, CONV_HALO, u[:, :d] * jax.nn.sigmoid(u[:, d:]))

    for j in range(d // LANES):
        lanes = slice(j * LANES, (j + 1) * LANES)
        w = wdw_ref[:, lanes]
        wb = [jnp.broadcast_to(w[k:k + 1, :], (CONV_ROWS, LANES)) for k in range(CONV_WIDTH)]
        bias = bdw_ref[:, lanes]
        for i in range(tm // CONV_ROWS):
            acc = None
            for k in range(CONV_WIDTH):
                start = CONV_HALO + i * CONV_ROWS - (CONV_WIDTH - 1) + k
                term = ubuf[j, _shift_rows(start, CONV_ROWS), :] * wb[k]
                acc = term if acc is None else acc + term
            ybuf[i * CONV_ROWS:(i + 1) * CONV_ROWS, lanes] = acc + bias

    _shift_carry_halo(ubuf, CONV_HALO, tm)

    v = ybuf[...]
    mu = jnp.mean(v, axis=-1, keepdims=True)
    vc = v - mu
    var = jnp.mean(vc * vc, axis=-1, keepdims=True)
    v = _silu(vc * lax.rsqrt(var + EPS) * lng_ref[...] + lnb_ref[...])
    y = jnp.dot(v.astype(jnp.bfloat16), w2_ref[...],
                preferred_element_type=jnp.float32) + b2_ref[...]
    o_ref[0] = x + (1.0 + m[2:3]) * y


def _conv_mixer(x, mod, g, w1, b1, wdw, bdw, lng, lnb, w2, b2):
    bsz, seq, d = x.shape
    tm = SEQ_TILE
    row = lambda a: a.reshape(1, -1)
    return pl.pallas_call(
        _conv_mixer_kernel,
        grid=(bsz, seq // tm),
        in_specs=[
            pl.BlockSpec((1, tm, d), lambda b, s: (b, s, 0)),
            pl.BlockSpec((1, N_ADA, d), lambda b, s: (b, 0, 0)),
            _const_spec((1, d)),
            _const_spec((d, 2 * d)),
            _const_spec((1, 2 * d)),
            _const_spec((CONV_WIDTH, d)),
            _const_spec((1, d)),
            _const_spec((1, d)),
            _const_spec((1, d)),
            _const_spec((d, d)),
            _const_spec((1, d)),
        ],
        out_specs=pl.BlockSpec((1, tm, d), lambda b, s: (b, s, 0)),
        out_shape=jax.ShapeDtypeStruct(x.shape, x.dtype),
        scratch_shapes=[
            _shift_buffer(CONV_HALO + tm, d),
            pltpu.VMEM((tm, d), jnp.float32),
        ],
        compiler_params=pltpu.CompilerParams(
            dimension_semantics=_ARB2, vmem_limit_bytes=VMEM_LIMIT),
        name="conv_mixer",
    )(x, mod, row(g), w1.astype(jnp.bfloat16), row(b1), wdw, row(bdw),
      row(lng), row(lnb), w2.astype(jnp.bfloat16), row(b2))


def _pool_mixer_kernel(x_ref, mod_ref, g_ref, pw_ref, ls_ref, o_ref, hbuf):
    tm, d = x_ref.shape[1], x_ref.shape[2]
    dg = d // len(POOL_WINDOWS)
    s = pl.program_id(1)

    @pl.when(s == 0)
    def _():
        _shift_zero_halo(hbuf, POOL_HALO)

    x = x_ref[0]
    m = mod_ref[0]
    h = _rms_mod(x, g_ref[...], m[0:1], m[1:2])
    _shift_store(hbuf, POOL_HALO, h)

    t = s * tm + lax.broadcasted_iota(jnp.int32, (tm, dg), 0)
    for gi, win in enumerate(POOL_WINDOWS):
        lanes = slice(gi * dg, (gi + 1) * dg)
        hg = h[:, lanes]
        wsum = hg
        for j in range(1, win):
            wsum = wsum + jnp.concatenate(
                [hbuf[sl, _shift_rows(POOL_HALO - j, tm), :]
                 for sl in range(gi * dg // LANES, (gi + 1) * dg // LANES)], axis=1)
        cnt = jnp.minimum(t + 1, win).astype(jnp.float32)
        mixed = wsum / cnt - hg
        y = jnp.dot(mixed.astype(jnp.bfloat16), pw_ref[gi],
                    preferred_element_type=jnp.float32) * ls_ref[:, lanes]
        o_ref[0, :, lanes] = x[:, lanes] + (1.0 + m[2:3, lanes]) * y

    _shift_carry_halo(hbuf, POOL_HALO, tm)


def _pool_mixer(x, mod, g, pw, ls):
    bsz, seq, d = x.shape
    tm = SEQ_TILE
    return pl.pallas_call(
        _pool_mixer_kernel,
        grid=(bsz, seq // tm),
        in_specs=[
            pl.BlockSpec((1, tm, d), lambda b, s: (b, s, 0)),
            pl.BlockSpec((1, N_ADA, d), lambda b, s: (b, 0, 0)),
            _const_spec((1, d)),
            _const_spec(pw.shape),
            _const_spec((1, d)),
        ],
        out_specs=pl.BlockSpec((1, tm, d), lambda b, s: (b, s, 0)),
        out_shape=jax.ShapeDtypeStruct(x.shape, x.dtype),
        scratch_shapes=[_shift_buffer(POOL_HALO + tm, d)],
        compiler_params=pltpu.CompilerParams(
            dimension_semantics=_ARB2, vmem_limit_bytes=VMEM_LIMIT),
        name="pool_mixer",
    )(x, mod, g.reshape(1, -1), pw.astype(jnp.bfloat16), ls.reshape(1, -1))


def _ffn_kernel(x_ref, mod_ref, g_ref, wg_ref, wu_ref, wd_ref, fg_ref, o_ref, *, final):
    x = x_ref[0]
    m = mod_ref[0]
    h = _rms_mod(x, g_ref[...], m[3:4], m[4:5]).astype(jnp.bfloat16)
    a = jnp.dot(h, wg_ref[...], preferred_element_type=jnp.float32)
    b = jnp.dot(h, wu_ref[...], preferred_element_type=jnp.float32)
    act = (_silu(a) * b).astype(jnp.bfloat16)
    y = jnp.dot(act, wd_ref[...], preferred_element_type=jnp.float32)
    out = x + (1.0 + m[5:6]) * y
    if final:
        ms = jnp.mean(out * out, axis=-1, keepdims=True)
        out = out * lax.rsqrt(ms + EPS) * fg_ref[...]
    o_ref[0] = out


def _ffn(x, mod, g, wg, wu, wd, fg, final):
    bsz, seq, d = x.shape
    f = wg.shape[1]
    tm = SEQ_TILE
    return pl.pallas_call(
        functools.partial(_ffn_kernel, final=final),
        grid=(bsz, seq // tm),
        in_specs=[
            pl.BlockSpec((1, tm, d), lambda b, s: (b, s, 0)),
            pl.BlockSpec((1, N_ADA, d), lambda b, s: (b, 0, 0)),
            _const_spec((1, d)),
            _const_spec((d, f)),
            _const_spec((d, f)),
            _const_spec((f, d)),
            _const_spec((1, d)),
        ],
        out_specs=pl.BlockSpec((1, tm, d), lambda b, s: (b, s, 0)),
        out_shape=jax.ShapeDtypeStruct(x.shape, x.dtype),
        compiler_params=pltpu.CompilerParams(
            dimension_semantics=_ARB2, vmem_limit_bytes=VMEM_LIMIT),
        name="ffn_final" if final else "ffn",
    )(x, mod, g.reshape(1, -1), wg.astype(jnp.bfloat16), wu.astype(jnp.bfloat16),
      wd.astype(jnp.bfloat16), fg.reshape(1, -1))


def kernel(x, c, ada_w, ada_b, norm_mix_g, norm_ffn_g, conv_w1, conv_b1, conv_wdw, conv_bdw, conv_ln_g, conv_ln_b, conv_w2, conv_b2, pool_w, pool_ls, ffn_w_gate, ffn_w_up, ffn_w_down, final_g):
    depth = ada_w.shape[0]
    mod = _ada_mod(c, ada_w, ada_b)
    for i in range(depth):
        j = i // 2
        if i % 2 == 0:
            x = _conv_mixer(x, mod[i], norm_mix_g[i], conv_w1[j], conv_b1[j], conv_wdw[j],
                            conv_bdw[j], conv_ln_g[j], conv_ln_b[j], conv_w2[j], conv_b2[j])
        else:
            x = _pool_mixer(x, mod[i], norm_mix_g[i], pool_w[j], pool_ls[j])
        x = _ffn(x, mod[i], norm_ffn_g[i], ffn_w_gate[i], ffn_w_up[i], ffn_w_down[i],
                 final_g, final=(i == depth - 1))
    return x
```

```python
import functools

import jax
import jax.numpy as jnp
from jax import lax
from jax.experimental import pallas as pl
from jax.experimental.pallas import tpu as pltpu

EPS = 1e-6
N_ADA = 6
CONV_WIDTH = 31
POOL_WINDOWS = (2, 4, 8, 16)

LANES = 128
SUBLANES = 8
MXU_COLS = 256
CONV_HALO = 32
POOL_HALO = 16
UNIT_ROWS = 64
SEQ_TILE = 512
LAG = 2
ADA_TILE = 1536
CONV_LAYER_SPLITS = dict(up=4, down=2, head=2)
POOL_LAYER_SPLITS = dict(up=2, down=1)
VMEM_LIMIT = 60 * 1024 * 1024


def _const_spec(shape):
    zeros = (0,) * len(shape)
    return pl.BlockSpec(shape, lambda *_: zeros, pipeline_mode=pl.Buffered(1))


def _rms_mod(x, g, shift, scale):
    ms = jnp.mean(x * x, axis=-1, keepdims=True)
    y = x * lax.rsqrt(ms + EPS) * g
    return y * (1.0 + scale) + shift


def _silu(x):
    return x * jax.nn.sigmoid(x)


ROW_PITCH = 2


def _shift_buffer(rows, d):
    return pltpu.VMEM((d // LANES, ROW_PITCH * rows, LANES), jnp.float32)


def _shift_rows(row0, n):
    return pl.ds(ROW_PITCH * row0, n, stride=ROW_PITCH)


def _shift_store(buf, row0, val):
    for j in range(val.shape[1] // LANES):
        buf[j, _shift_rows(row0, val.shape[0]), :] = val[:, j * LANES:(j + 1) * LANES]


def _shift_zero_halo(buf, halo):
    for j in range(buf.shape[0]):
        buf[j, _shift_rows(0, halo), :] = jnp.zeros((halo, LANES), jnp.float32)


def _shift_carry_halo(buf, halo, tm):
    for j in range(buf.shape[0]):
        buf[j, _shift_rows(0, halo), :] = buf[j, _shift_rows(tm, halo), :]


def _ada_kernel(c_ref, w_ref, b_ref, o_ref):
    ca = _silu(c_ref[...])
    o_ref[0] = jnp.dot(ca, w_ref[0], preferred_element_type=jnp.float32) + b_ref[0]


def _ada_mod(c, ada_w, ada_b):
    depth, d, n = ada_w.shape
    bsz = c.shape[0]
    rows = -(-bsz // SUBLANES) * SUBLANES
    c_pad = jnp.zeros((rows, d), c.dtype).at[:bsz].set(c)
    out = pl.pallas_call(
        _ada_kernel,
        grid=(depth, n // ADA_TILE),
        in_specs=[
            pl.BlockSpec((rows, d), lambda i, j: (0, 0)),
            pl.BlockSpec((1, d, ADA_TILE), lambda i, j: (i, 0, j)),
            pl.BlockSpec((1, 1, ADA_TILE), lambda i, j: (i, 0, j)),
        ],
        out_specs=pl.BlockSpec((1, rows, ADA_TILE), lambda i, j: (i, 0, j)),
        out_shape=jax.ShapeDtypeStruct((depth, rows, n), jnp.float32),
        compiler_params=pltpu.CompilerParams(
            dimension_semantics=("arbitrary", "arbitrary"), vmem_limit_bytes=VMEM_LIMIT),
        name="ada_mod",
    )(c_pad, ada_w, ada_b.reshape(depth, 1, n))
    return out[:, :bsz].reshape(depth, bsz, N_ADA, d)


def _zero_of(anchor):
    bits = pltpu.bitcast(anchor, jnp.uint32)
    return pltpu.bitcast((bits >> 16) >> 16, jnp.float32)


def _anchor(val):
    return val[0:SUBLANES, 0:LANES]


def _operand_after(ref, anchor):
    x = ref[...]
    if anchor is None:
        return x
    zero = _zero_of(anchor)
    top = (x[0:2 * SUBLANES, 0:LANES].astype(jnp.float32)
           + jnp.concatenate([zero, zero], axis=0)).astype(ref.dtype)
    top_rows = jnp.concatenate([top, x[0:2 * SUBLANES, LANES:]], axis=1)
    return jnp.concatenate([top_rows, x[2 * SUBLANES:]], axis=0)


def _deal(items, n):
    return [items[len(items) * k // n:len(items) * (k + 1) // n] for k in range(n)]


def _col_block(c):
    return slice(c * MXU_COLS, (c + 1) * MXU_COLS)


def _interleave_cols(a, b):
    lead, nb = a.shape[:-1], a.shape[-1] // MXU_COLS
    return jnp.stack([a.reshape(*lead, nb, MXU_COLS), b.reshape(*lead, nb, MXU_COLS)],
                     axis=-2).reshape(*lead, 2 * a.shape[-1])


def _first_step_init(n, *bufs):
    @pl.when(n == 0)
    def _():
        for buf in bufs:
            buf[...] = jnp.zeros(buf.shape, buf.dtype)


def _mixer_residual(xlag_ref, m, y, gf_ref, o_ref, hb):
    xm = xlag_ref[0] + (1.0 + m[2:3]) * y
    o_ref[0] = xm
    hb[...] = _rms_mod(xm, gf_ref[...], m[3:4], m[4:5]).astype(jnp.bfloat16)


def _ffn_up_blocks(hb, wgu_ref, act, blocks, after):
    c0 = blocks[0]
    ab = jnp.dot(_operand_after(hb, after),
                 wgu_ref[:, 2 * c0 * MXU_COLS:2 * (blocks[-1] + 1) * MXU_COLS],
                 preferred_element_type=jnp.float32)
    for c in blocks:
        prod = _silu(ab[:, _col_block(2 * (c - c0))]) * ab[:, _col_block(2 * (c - c0) + 1)]
        act[:, _col_block(c)] = prod.astype(jnp.bfloat16)
    return _anchor(prod)


def _ffn_down_blocks(act, m, wd_ref, o_ref, blocks, after):
    cols = slice(blocks[0] * MXU_COLS, (blocks[-1] + 1) * MXU_COLS)
    y = jnp.dot(_operand_after(act, after), wd_ref[:, cols], preferred_element_type=jnp.float32)
    o_ref[0, :, cols] = o_ref[0, :, cols] + (1.0 + m[5:6, cols]) * y
    return _anchor(y)


def _final_norm(o_ref, fg_ref):
    out = o_ref[0]
    ms = jnp.mean(out * out, axis=-1, keepdims=True)
    o_ref[0] = out * lax.rsqrt(ms + EPS) * fg_ref[...]


def _conv_units(units, wdw_ref, bdw_ref, ubuf, ybuf, after):
    taps = {}
    for j, i in units:
        lanes = slice(j * LANES, (j + 1) * LANES)
        if j not in taps:
            w = wdw_ref[:, lanes]
            taps = {j: [jnp.broadcast_to(w[k:k + 1, :], (UNIT_ROWS, LANES))
                        for k in range(CONV_WIDTH)]}
        acc = pltpu.repeat(_zero_of(after), UNIT_ROWS // SUBLANES, axis=0)
        for k in range(CONV_WIDTH):
            start = CONV_HALO + i * UNIT_ROWS - (CONV_WIDTH - 1) + k
            acc = acc + ubuf[j, _shift_rows(start, UNIT_ROWS), :] * taps[j][k]
        ybuf[i * UNIT_ROWS:(i + 1) * UNIT_ROWS, lanes] = acc + bdw_ref[:, lanes]
        after = _anchor(acc)
    return after


def _conv_layer_kernel(x_ref, xlag_ref, mod_ref, modlag_ref, gm_ref, w1_ref, b1_ref, wdw_ref,
                       bdw_ref, lng_ref, lnb_ref, w2_ref, b2_ref, gf_ref, wgu_ref, wd_ref,
                       fg_ref, o_ref, hb, act, ha, gbuf, ubuf, ybuf, *, tiles_per_seq, final):
    n = pl.program_id(0)
    tm, d = x_ref.shape[1], x_ref.shape[2]
    f = act.shape[1]
    up_groups = _deal(list(range(f // MXU_COLS)), CONV_LAYER_SPLITS["up"])
    down_groups = _deal(list(range(d // MXU_COLS)), CONV_LAYER_SPLITS["down"])
    head_groups = _deal(list(range(d // MXU_COLS)), CONV_LAYER_SPLITS["head"])
    n_matmuls = len(up_groups) + len(down_groups) + len(head_groups)
    units = [(j, i) for j in range(d // LANES) for i in range(tm // UNIT_ROWS)]
    unit_groups = iter(_deal(units, n_matmuls - 1))
    _first_step_init(n, gbuf, ybuf)

    @pl.when((n == 0) | (lax.rem(n + tiles_per_seq - 1, tiles_per_seq) == 0))
    def _():
        _shift_zero_halo(ubuf, CONV_HALO)

    _shift_store(ubuf, CONV_HALO, gbuf[...])
    v = ybuf[...]
    mu = jnp.mean(v, axis=-1, keepdims=True)
    vc = v - mu
    var = jnp.mean(vc * vc, axis=-1, keepdims=True)
    v = _silu(vc * lax.rsqrt(var + EPS) * lng_ref[...] + lnb_ref[...])
    y = jnp.dot(v.astype(jnp.bfloat16), w2_ref[...],
                preferred_element_type=jnp.float32) + b2_ref[...]
    _mixer_residual(xlag_ref, modlag_ref[0], y, gf_ref, o_ref, hb)
    m = mod_ref[0]
    ha[...] = _rms_mod(x_ref[0], gm_ref[...], m[0:1], m[1:2]).astype(jnp.bfloat16)

    done, ready = _anchor(y), None
    for blocks in up_groups:
        done_next = _ffn_up_blocks(hb, wgu_ref, act, blocks, ready)
        ready = _conv_units(next(unit_groups), wdw_ref, bdw_ref, ubuf, ybuf, done)
        done = done_next
    for blocks in down_groups:
        done_next = _ffn_down_blocks(act, modlag_ref[0], wd_ref, o_ref, blocks, ready)
        ready = _conv_units(next(unit_groups), wdw_ref, bdw_ref, ubuf, ybuf, done)
        done = done_next
    for k, blocks in enumerate(head_groups):
        c0 = blocks[0]
        cols = slice(2 * c0 * MXU_COLS, 2 * (blocks[-1] + 1) * MXU_COLS)
        u = jnp.dot(_operand_after(ha, ready), w1_ref[:, cols],
                    preferred_element_type=jnp.float32) + b1_ref[:, cols]
        for c in blocks:
            gbuf[:, _col_block(c)] = (u[:, _col_block(2 * (c - c0))]
                                      * jax.nn.sigmoid(u[:, _col_block(2 * (c - c0) + 1)]))
        if k + 1 < len(head_groups):
            ready = _conv_units(next(unit_groups), wdw_ref, bdw_ref, ubuf, ybuf, done)
            done = _anchor(u)
    _shift_carry_halo(ubuf, CONV_HALO, tm)
    if final:
        _final_norm(o_ref, fg_ref)


def _pool_units(chunks, seq_row0, hbuf, mixbuf, after):
    slabs_per_group = hbuf.shape[0] // len(POOL_WINDOWS)
    for i in chunks:
        row0 = i * UNIT_ROWS
        zero = pltpu.repeat(_zero_of(after), UNIT_ROWS // SUBLANES, axis=0)
        t = seq_row0 + row0 + lax.broadcasted_iota(jnp.int32, (UNIT_ROWS, LANES), 0)
        for gi, win in enumerate(POOL_WINDOWS):
            cnt = jnp.minimum(t + 1, win).astype(jnp.float32)
            for sl in range(gi * slabs_per_group, (gi + 1) * slabs_per_group):
                h = hbuf[sl, _shift_rows(POOL_HALO + row0, UNIT_ROWS), :]
                wsum = h + zero
                for j in range(1, win):
                    wsum = wsum + hbuf[sl, _shift_rows(POOL_HALO + row0 - j, UNIT_ROWS), :]
                mixbuf[row0:row0 + UNIT_ROWS, sl * LANES:(sl + 1) * LANES] = (
                    wsum / cnt - h).astype(jnp.bfloat16)
        after = _anchor(wsum)
    return after


def _pool_layer_kernel(x_ref, xlag_ref, mod_ref, modlag_ref, gm_ref, pw_ref, ls_ref, gf_ref,
                       wgu_ref, wd_ref, fg_ref, o_ref, hb, act, hplain, hbuf, mixbuf, *,
                       tiles_per_seq, final):
    n = pl.program_id(0)
    tm, d = x_ref.shape[1], x_ref.shape[2]
    f = act.shape[1]
    dg = d // len(POOL_WINDOWS)
    window_tile = lax.rem(n + tiles_per_seq - 1, tiles_per_seq)
    up_groups = _deal(list(range(f // MXU_COLS)), POOL_LAYER_SPLITS["up"])
    down_groups = _deal(list(range(d // MXU_COLS)), POOL_LAYER_SPLITS["down"])
    n_matmuls = len(up_groups) + len(down_groups)
    chunk_groups = _deal(list(range(tm // UNIT_ROWS)), n_matmuls - 1)
    _first_step_init(n, hplain, mixbuf)

    @pl.when((n == 0) | (window_tile == 0))
    def _():
        _shift_zero_halo(hbuf, POOL_HALO)

    _shift_store(hbuf, POOL_HALO, hplain[...])
    ys = [jnp.dot(mixbuf[:, gi * dg:(gi + 1) * dg], pw_ref[gi],
                  preferred_element_type=jnp.float32) for gi in range(len(POOL_WINDOWS))]
    y = jnp.concatenate(ys, axis=1) * ls_ref[...]
    _mixer_residual(xlag_ref, modlag_ref[0], y, gf_ref, o_ref, hb)

    done, ready = _anchor(y), None
    matmuls = ([functools.partial(_ffn_up_blocks, hb, wgu_ref, act, blocks)
                for blocks in up_groups]
               + [functools.partial(_ffn_down_blocks, act, modlag_ref[0], wd_ref, o_ref, blocks)
                  for blocks in down_groups])
    for k, matmul in enumerate(matmuls):
        done_next = matmul(ready)
        if k < len(chunk_groups):
            ready = _pool_units(chunk_groups[k], window_tile * tm, hbuf, mixbuf, done)
            done = done_next
    _shift_carry_halo(hbuf, POOL_HALO, tm)
    if final:
        _final_norm(o_ref, fg_ref)
    m = mod_ref[0]
    hplain[...] = _rms_mod(x_ref[0], gm_ref[...], m[0:1], m[1:2])


def _layer_call(body, name, x, mod, consts, scratch, final):
    bsz, seq, d = x.shape
    tm = SEQ_TILE
    tiles_per_seq = seq // tm
    n_tiles = bsz * tiles_per_seq
    head_tile = lambda n: jnp.minimum(n, n_tiles - 1)
    tail_tile = lambda n: jnp.maximum(n - LAG, 0)
    xt = x.reshape(n_tiles, tm, d)
    out = pl.pallas_call(
        functools.partial(body, tiles_per_seq=tiles_per_seq, final=final),
        grid=(n_tiles + LAG,),
        in_specs=[
            pl.BlockSpec((1, tm, d), lambda n: (head_tile(n), 0, 0)),
            pl.BlockSpec((1, tm, d), lambda n: (tail_tile(n), 0, 0)),
            pl.BlockSpec((1, N_ADA, d), lambda n: (head_tile(n) // tiles_per_seq, 0, 0)),
            pl.BlockSpec((1, N_ADA, d), lambda n: (tail_tile(n) // tiles_per_seq, 0, 0)),
        ] + [_const_spec(a.shape) for a in consts],
        out_specs=pl.BlockSpec((1, tm, d), lambda n: (tail_tile(n), 0, 0)),
        out_shape=jax.ShapeDtypeStruct((n_tiles, tm, d), x.dtype),
        scratch_shapes=list(scratch),
        compiler_params=pltpu.CompilerParams(
            dimension_semantics=("arbitrary",), vmem_limit_bytes=VMEM_LIMIT),
        name=name,
    )(xt, xt, mod, mod, *consts)
    return out.reshape(bsz, seq, d)


def kernel(x, c, ada_w, ada_b, norm_mix_g, norm_ffn_g, conv_w1, conv_b1, conv_wdw, conv_bdw, conv_ln_g, conv_ln_b, conv_w2, conv_b2, pool_w, pool_ls, ffn_w_gate, ffn_w_up, ffn_w_down, final_g):
    depth = ada_w.shape[0]
    d = x.shape[-1]
    f = ffn_w_gate.shape[-1]
    row = lambda a: a.reshape(1, -1)
    bf16 = lambda a: a.astype(jnp.bfloat16)
    mod = _ada_mod(c, ada_w, ada_b)
    ffn_scratch = (pltpu.VMEM((SEQ_TILE, d), jnp.bfloat16), pltpu.VMEM((SEQ_TILE, f), jnp.bfloat16))
    for i in range(depth):
        j = i // 2
        final = i == depth - 1
        wgu = _interleave_cols(bf16(ffn_w_gate[i]), bf16(ffn_w_up[i]))
        ffn_consts = (row(norm_ffn_g[i]), wgu, bf16(ffn_w_down[i]), row(final_g))
        if i % 2 == 0:
            w1 = _interleave_cols(bf16(conv_w1[j][:, :d]), bf16(conv_w1[j][:, d:]))
            b1 = _interleave_cols(conv_b1[j][:d], conv_b1[j][d:])
            consts = (row(norm_mix_g[i]), w1, row(b1), conv_wdw[j],
                      row(conv_bdw[j]), row(conv_ln_g[j]), row(conv_ln_b[j]),
                      bf16(conv_w2[j]), row(conv_b2[j])) + ffn_consts
            scratch = ffn_scratch + (pltpu.VMEM((SEQ_TILE, d), jnp.bfloat16),
                                     pltpu.VMEM((SEQ_TILE, d), jnp.float32),
                                     _shift_buffer(CONV_HALO + SEQ_TILE, d),
                                     pltpu.VMEM((SEQ_TILE, d), jnp.float32))
            x = _layer_call(_conv_layer_kernel, "conv_layer", x, mod[i], consts, scratch, final)
        else:
            consts = (row(norm_mix_g[i]), bf16(pool_w[j]), row(pool_ls[j])) + ffn_consts
            scratch = ffn_scratch + (pltpu.VMEM((SEQ_TILE, d), jnp.float32),
                                     _shift_buffer(POOL_HALO + SEQ_TILE, d),
                                     pltpu.VMEM((SEQ_TILE, d), jnp.bfloat16))
            x = _layer_call(_pool_layer_kernel, "pool_layer", x, mod[i], consts, scratch, final)
    return x
```

```python
import functools

import jax
import jax.numpy as jnp
from jax import lax
from jax.experimental import pallas as pl
from jax.experimental.pallas import tpu as pltpu

EPS = 1e-6
N_ADA = 6
CONV_WIDTH = 31
POOL_WINDOWS = (2, 4, 8, 16)

LANES = 128
SUBLANES = 8
MXU_COLS = 256
CONV_HALO = 32
POOL_HALO = 16
UNIT_ROWS = 64
SEQ_TILE = 512
LAG = 2
ADA_TILE = 1536
CONV_LAYER_SPLITS = dict(up=4, down=2, head=2)
POOL_LAYER_SPLITS = dict(up=2, down=1)
VMEM_LIMIT = 60 * 1024 * 1024


def _const_spec(a, layer=None):
    if layer is None:
        zeros = (0,) * a.ndim
        return pl.BlockSpec(a.shape, lambda *_: zeros, pipeline_mode=pl.Buffered(1))
    index = (layer,) + (0,) * (a.ndim - 1)
    return pl.BlockSpec((None,) + a.shape[1:], lambda *_: index, pipeline_mode=pl.Buffered(1))


def _rms_mod(x, g, shift, scale):
    ms = jnp.mean(x * x, axis=-1, keepdims=True)
    y = x * lax.rsqrt(ms + EPS) * g
    return y * (1.0 + scale) + shift


def _silu(x):
    return x * jax.nn.sigmoid(x)


ROW_PITCH = 2


def _shift_buffer(rows, d):
    return pltpu.VMEM((d // LANES, ROW_PITCH * rows, LANES), jnp.float32)


def _shift_rows(row0, n):
    return pl.ds(ROW_PITCH * row0, n, stride=ROW_PITCH)


def _shift_store(buf, row0, val):
    for j in range(val.shape[1] // LANES):
        buf[j, _shift_rows(row0, val.shape[0]), :] = val[:, j * LANES:(j + 1) * LANES]


def _shift_zero_halo(buf, halo):
    for j in range(buf.shape[0]):
        buf[j, _shift_rows(0, halo), :] = jnp.zeros((halo, LANES), jnp.float32)


def _shift_carry_halo(buf, halo, tm):
    for j in range(buf.shape[0]):
        buf[j, _shift_rows(0, halo), :] = buf[j, _shift_rows(tm, halo), :]


def _ada_kernel(c_ref, w_ref, b_ref, o_ref):
    ca = _silu(c_ref[...])
    o_ref[0] = jnp.dot(ca, w_ref[0], preferred_element_type=jnp.float32) + b_ref[0]


def _ada_mod(c, ada_w, ada_b):
    depth, d, n = ada_w.shape
    bsz = c.shape[0]
    rows = -(-bsz // SUBLANES) * SUBLANES
    c_pad = jnp.zeros((rows, d), c.dtype).at[:bsz].set(c)
    out = pl.pallas_call(
        _ada_kernel,
        grid=(depth, n // ADA_TILE),
        in_specs=[
            pl.BlockSpec((rows, d), lambda i, j: (0, 0)),
            pl.BlockSpec((1, d, ADA_TILE), lambda i, j: (i, 0, j)),
            pl.BlockSpec((1, 1, ADA_TILE), lambda i, j: (i, 0, j)),
        ],
        out_specs=pl.BlockSpec((1, rows, ADA_TILE), lambda i, j: (i, 0, j)),
        out_shape=jax.ShapeDtypeStruct((depth, rows, n), jnp.float32),
        compiler_params=pltpu.CompilerParams(
            dimension_semantics=("arbitrary", "arbitrary"), vmem_limit_bytes=VMEM_LIMIT),
        name="ada_mod",
    )(c_pad, ada_w, ada_b.reshape(depth, 1, n))
    return out[:, :bsz].reshape(depth, bsz, N_ADA, d)


def _zero_of(anchor):
    bits = pltpu.bitcast(anchor, jnp.uint32)
    return pltpu.bitcast((bits >> 16) >> 16, jnp.float32)


def _anchor(val):
    return val[0:SUBLANES, 0:LANES]


def _operand_after(ref, anchor):
    x = ref[...]
    if anchor is None:
        return x
    zero = _zero_of(anchor)
    top = (x[0:2 * SUBLANES, 0:LANES].astype(jnp.float32)
           + jnp.concatenate([zero, zero], axis=0)).astype(ref.dtype)
    top_rows = jnp.concatenate([top, x[0:2 * SUBLANES, LANES:]], axis=1)
    return jnp.concatenate([top_rows, x[2 * SUBLANES:]], axis=0)


def _deal(items, n):
    return [items[len(items) * k // n:len(items) * (k + 1) // n] for k in range(n)]


def _block_cols(blocks):
    return slice(blocks[0] * MXU_COLS, (blocks[-1] + 1) * MXU_COLS)


def _after(x, anchor):
    reps = (x.shape[0] // SUBLANES, x.shape[1] // LANES)
    return x + jnp.tile(_zero_of(anchor), reps)


def _first_step_init(n, *bufs):
    @pl.when(n == 0)
    def _():
        for buf in bufs:
            buf[...] = jnp.zeros(buf.shape, buf.dtype)


def _mixer_residual(xlag_ref, m, y, gf_ref, o_ref, hb):
    xm = xlag_ref[0] + (1.0 + m[2:3]) * y
    o_ref[0] = xm
    hb[...] = _rms_mod(xm, gf_ref[...], m[3:4], m[4:5]).astype(jnp.bfloat16)


def _ffn_up_blocks(hb, wg_ref, wu_ref, act, blocks, after):
    cols = _block_cols(blocks)
    n = cols.stop - cols.start
    w = jnp.concatenate([wg_ref[:, cols], wu_ref[:, cols]], axis=1)
    ab = jnp.dot(_operand_after(hb, after), w, preferred_element_type=jnp.float32)
    prod = _silu(ab[:, :n]) * ab[:, n:]
    act[:, cols] = prod.astype(jnp.bfloat16)
    return _anchor(prod)


def _ffn_down_blocks(act, m, wd_ref, o_ref, blocks, after):
    cols = _block_cols(blocks)
    y = jnp.dot(_operand_after(act, after), wd_ref[:, cols], preferred_element_type=jnp.float32)
    o_ref[0, :, cols] = o_ref[0, :, cols] + (1.0 + m[5:6, cols]) * y
    return _anchor(y)


def _final_norm(o_ref, fg_ref):
    out = o_ref[0]
    ms = jnp.mean(out * out, axis=-1, keepdims=True)
    o_ref[0] = out * lax.rsqrt(ms + EPS) * fg_ref[...]


def _conv_units(units, wdw_ref, bdw_ref, ubuf, ybuf, after):
    taps = {}
    for j, i in units:
        lanes = slice(j * LANES, (j + 1) * LANES)
        if j not in taps:
            w = wdw_ref[:, lanes]
            taps = {j: [jnp.broadcast_to(w[k:k + 1, :], (UNIT_ROWS, LANES))
                        for k in range(CONV_WIDTH)]}
        acc = pltpu.repeat(_zero_of(after), UNIT_ROWS // SUBLANES, axis=0)
        for k in range(CONV_WIDTH):
            start = CONV_HALO + i * UNIT_ROWS - (CONV_WIDTH - 1) + k
            acc = acc + ubuf[j, _shift_rows(start, UNIT_ROWS), :] * taps[j][k]
        ybuf[i * UNIT_ROWS:(i + 1) * UNIT_ROWS, lanes] = acc + bdw_ref[:, lanes]
        after = _anchor(acc)
    return after


def _conv_layer_kernel(x_ref, xlag_ref, mod_ref, modlag_ref, gm_ref, w1_ref, b1_ref, wdw_ref,
                       bdw_ref, lng_ref, lnb_ref, w2_ref, b2_ref, gf_ref, wg_ref, wu_ref, wd_ref,
                       fg_ref, o_ref, hb, act, ha, gbuf, ubuf, ybuf, *, tiles_per_seq, final):
    n = pl.program_id(0)
    tm, d = x_ref.shape[1], x_ref.shape[2]
    f = act.shape[1]
    up_groups = _deal(list(range(f // MXU_COLS)), CONV_LAYER_SPLITS["up"])
    down_groups = _deal(list(range(d // MXU_COLS)), CONV_LAYER_SPLITS["down"])
    head_groups = _deal(list(range(d // MXU_COLS)), CONV_LAYER_SPLITS["head"])
    n_matmuls = len(up_groups) + len(down_groups) + len(head_groups)
    units = [(j, i) for j in range(d // LANES) for i in range(tm // UNIT_ROWS)]
    unit_groups = iter(_deal(units, n_matmuls - 1))
    _first_step_init(n, gbuf, ybuf)

    @pl.when((n == 0) | (lax.rem(n + tiles_per_seq - 1, tiles_per_seq) == 0))
    def _():
        _shift_zero_halo(ubuf, CONV_HALO)

    _shift_store(ubuf, CONV_HALO, gbuf[...])
    v = ybuf[...]
    mu = jnp.mean(v, axis=-1, keepdims=True)
    vc = v - mu
    var = jnp.mean(vc * vc, axis=-1, keepdims=True)
    v = _silu(vc * lax.rsqrt(var + EPS) * lng_ref[...] + lnb_ref[...])
    y = jnp.dot(v.astype(jnp.bfloat16), w2_ref[...],
                preferred_element_type=jnp.float32) + b2_ref[...]
    _mixer_residual(xlag_ref, modlag_ref[0], y, gf_ref, o_ref, hb)

    done, ready = _anchor(y), None
    for k, blocks in enumerate(up_groups):
        done_next = _ffn_up_blocks(hb, wg_ref, wu_ref, act, blocks, ready)
        ready = _conv_units(next(unit_groups), wdw_ref, bdw_ref, ubuf, ybuf, done)
        if k == 0:
            m = mod_ref[0]
            ha[...] = _rms_mod(_after(x_ref[0], ready), gm_ref[...], m[0:1],
                               m[1:2]).astype(jnp.bfloat16)
        done = done_next
    for blocks in down_groups:
        done_next = _ffn_down_blocks(act, modlag_ref[0], wd_ref, o_ref, blocks, ready)
        ready = _conv_units(next(unit_groups), wdw_ref, bdw_ref, ubuf, ybuf, done)
        done = done_next
    for k, blocks in enumerate(head_groups):
        cols = _block_cols(blocks)
        gate_cols = slice(d + cols.start, d + cols.stop)
        nc = cols.stop - cols.start
        w = jnp.concatenate([w1_ref[:, cols], w1_ref[:, gate_cols]], axis=1)
        b = jnp.concatenate([b1_ref[:, cols], b1_ref[:, gate_cols]], axis=1)
        u = jnp.dot(_operand_after(ha, ready), w, preferred_element_type=jnp.float32) + b
        gbuf[:, cols] = u[:, :nc] * jax.nn.sigmoid(u[:, nc:])
        if k + 1 < len(head_groups):
            ready = _conv_units(next(unit_groups), wdw_ref, bdw_ref, ubuf, ybuf, done)
            done = _anchor(u)
    _shift_carry_halo(ubuf, CONV_HALO, tm)
    if final:
        _final_norm(o_ref, fg_ref)


def _pool_units(chunks, seq_row0, hbuf, mixbuf, after):
    slabs_per_group = hbuf.shape[0] // len(POOL_WINDOWS)
    for i in chunks:
        row0 = i * UNIT_ROWS
        zero = pltpu.repeat(_zero_of(after), UNIT_ROWS // SUBLANES, axis=0)
        t = seq_row0 + row0 + lax.broadcasted_iota(jnp.int32, (UNIT_ROWS, LANES), 0)
        for gi, win in enumerate(POOL_WINDOWS):
            cnt = jnp.minimum(t + 1, win).astype(jnp.float32)
            for sl in range(gi * slabs_per_group, (gi + 1) * slabs_per_group):
                h = hbuf[sl, _shift_rows(POOL_HALO + row0, UNIT_ROWS), :]
                wsum = h + zero
                for j in range(1, win):
                    wsum = wsum + hbuf[sl, _shift_rows(POOL_HALO + row0 - j, UNIT_ROWS), :]
                mixbuf[row0:row0 + UNIT_ROWS, sl * LANES:(sl + 1) * LANES] = (
                    wsum / cnt - h).astype(jnp.bfloat16)
        after = _anchor(wsum)
    return after


def _pool_layer_kernel(x_ref, xlag_ref, mod_ref, modlag_ref, gm_ref, pw_ref, ls_ref, gf_ref,
                       wg_ref, wu_ref, wd_ref, fg_ref, o_ref, hb, act, hplain, hbuf, mixbuf, *,
                       tiles_per_seq, final):
    n = pl.program_id(0)
    tm, d = x_ref.shape[1], x_ref.shape[2]
    f = act.shape[1]
    dg = d // len(POOL_WINDOWS)
    window_tile = lax.rem(n + tiles_per_seq - 1, tiles_per_seq)
    up_groups = _deal(list(range(f // MXU_COLS)), POOL_LAYER_SPLITS["up"])
    down_groups = _deal(list(range(d // MXU_COLS)), POOL_LAYER_SPLITS["down"])
    n_matmuls = len(up_groups) + len(down_groups)
    chunk_groups = _deal(list(range(tm // UNIT_ROWS)), n_matmuls - 1)
    _first_step_init(n, hplain, mixbuf)

    @pl.when((n == 0) | (window_tile == 0))
    def _():
        _shift_zero_halo(hbuf, POOL_HALO)

    _shift_store(hbuf, POOL_HALO, hplain[...])
    ys = [jnp.dot(mixbuf[:, gi * dg:(gi + 1) * dg], pw_ref[gi],
                  preferred_element_type=jnp.float32) for gi in range(len(POOL_WINDOWS))]
    y = jnp.concatenate(ys, axis=1) * ls_ref[...]
    _mixer_residual(xlag_ref, modlag_ref[0], y, gf_ref, o_ref, hb)

    done, ready = _anchor(y), None
    matmuls = ([functools.partial(_ffn_up_blocks, hb, wg_ref, wu_ref, act, blocks)
                for blocks in up_groups]
               + [functools.partial(_ffn_down_blocks, act, modlag_ref[0], wd_ref, o_ref, blocks)
                  for blocks in down_groups])
    for k, matmul in enumerate(matmuls):
        done_next = matmul(ready)
        if k < len(chunk_groups):
            ready = _pool_units(chunk_groups[k], window_tile * tm, hbuf, mixbuf, done)
            done = done_next
        if k == len(chunk_groups) - 1:
            m = mod_ref[0]
            h = _rms_mod(_after(x_ref[0], ready), gm_ref[...], m[0:1], m[1:2])
            hplain[...] = h
            ready = _anchor(h)
    _shift_carry_halo(hbuf, POOL_HALO, tm)
    if final:
        _final_norm(o_ref, fg_ref)


def _layer_call(body, name, x, mod, consts, scratch, final):
    bsz, seq, d = x.shape
    tm = SEQ_TILE
    tiles_per_seq = seq // tm
    n_tiles = bsz * tiles_per_seq
    head_tile = lambda n: jnp.minimum(n, n_tiles - 1)
    tail_tile = lambda n: jnp.maximum(n - LAG, 0)
    xt = x.reshape(n_tiles, tm, d)
    out = pl.pallas_call(
        functools.partial(body, tiles_per_seq=tiles_per_seq, final=final),
        grid=(n_tiles + LAG,),
        in_specs=[
            pl.BlockSpec((1, tm, d), lambda n: (head_tile(n), 0, 0)),
            pl.BlockSpec((1, tm, d), lambda n: (tail_tile(n), 0, 0)),
            pl.BlockSpec((1, N_ADA, d), lambda n: (head_tile(n) // tiles_per_seq, 0, 0)),
            pl.BlockSpec((1, N_ADA, d), lambda n: (tail_tile(n) // tiles_per_seq, 0, 0)),
        ] + [_const_spec(a, layer) for a, layer in consts],
        out_specs=pl.BlockSpec((1, tm, d), lambda n: (tail_tile(n), 0, 0)),
        out_shape=jax.ShapeDtypeStruct((n_tiles, tm, d), x.dtype),
        scratch_shapes=list(scratch),
        compiler_params=pltpu.CompilerParams(
            dimension_semantics=("arbitrary",), vmem_limit_bytes=VMEM_LIMIT),
        name=name,
    )(xt, xt, mod, mod, *[a for a, _ in consts])
    return out.reshape(bsz, seq, d)


def kernel(x, c, ada_w, ada_b, norm_mix_g, norm_ffn_g, conv_w1, conv_b1, conv_wdw, conv_bdw, conv_ln_g, conv_ln_b, conv_w2, conv_b2, pool_w, pool_ls, ffn_w_gate, ffn_w_up, ffn_w_down, final_g):
    depth = ada_w.shape[0]
    d = x.shape[-1]
    f = ffn_w_gate.shape[-1]
    whole = lambda a: (a.reshape(1, -1), None)
    bf16 = lambda a: a.astype(jnp.bfloat16)
    wg, wu, wd = bf16(ffn_w_gate), bf16(ffn_w_up), bf16(ffn_w_down)
    w1, w2, pw = bf16(conv_w1), bf16(conv_w2), bf16(pool_w)
    mod = _ada_mod(c, ada_w, ada_b)
    ffn_scratch = (pltpu.VMEM((SEQ_TILE, d), jnp.bfloat16), pltpu.VMEM((SEQ_TILE, f), jnp.bfloat16))
    for i in range(depth):
        j = i // 2
        final = i == depth - 1
        ffn_consts = (whole(norm_ffn_g[i]), (wg, i), (wu, i), (wd, i), whole(final_g))
        if i % 2 == 0:
            consts = (whole(norm_mix_g[i]), (w1, j), whole(conv_b1[j]), (conv_wdw, j),
                      whole(conv_bdw[j]), whole(conv_ln_g[j]), whole(conv_ln_b[j]),
                      (w2, j), whole(conv_b2[j])) + ffn_consts
            scratch = ffn_scratch + (pltpu.VMEM((SEQ_TILE, d), jnp.bfloat16),
                                     pltpu.VMEM((SEQ_TILE, d), jnp.float32),
                                     _shift_buffer(CONV_HALO + SEQ_TILE, d),
                                     pltpu.VMEM((SEQ_TILE, d), jnp.float32))
            x = _layer_call(_conv_layer_kernel, "conv_layer", x, mod[i], consts, scratch, final)
        else:
            consts = (whole(norm_mix_g[i]), (pw, j), whole(pool_ls[j])) + ffn_consts
            scratch = ffn_scratch + (pltpu.VMEM((SEQ_TILE, d), jnp.float32),
                                     _shift_buffer(POOL_HALO + SEQ_TILE, d),
                                     pltpu.VMEM((SEQ_TILE, d), jnp.bfloat16))
            x = _layer_call(_pool_layer_kernel, "pool_layer", x, mod[i], consts, scratch, final)
    return x
```

```python
import functools

import jax
import jax.numpy as jnp
from jax import lax
from jax.experimental import pallas as pl
from jax.experimental.pallas import tpu as pltpu

EPS = 1e-6
N_ADA = 6
CONV_WIDTH = 31
POOL_WINDOWS = (2, 4, 8, 16)

LANES = 128
SUBLANES = 8
MXU_COLS = 256
CONV_HALO = 32
POOL_HALO = 16
UNIT_ROWS = 64
SEQ_TILE = 512
LAG = 2
ADA_TILE = 1536
CONV_LAYER_SPLITS = dict(up=4, down=2, head=2)
POOL_LAYER_SPLITS = dict(up=2, down=1)
VMEM_LIMIT = 60 * 1024 * 1024


def _const_spec(a, layer=None):
    if layer is None:
        zeros = (0,) * a.ndim
        return pl.BlockSpec(a.shape, lambda *_: zeros, pipeline_mode=pl.Buffered(1))
    index = (layer,) + (0,) * (a.ndim - 1)
    return pl.BlockSpec((None,) + a.shape[1:], lambda *_: index, pipeline_mode=pl.Buffered(1))


def _rms_mod(x, g, shift, scale):
    ms = jnp.mean(x * x, axis=-1, keepdims=True)
    y = x * lax.rsqrt(ms + EPS) * g
    return y * (1.0 + scale) + shift


def _silu(x):
    return x * jax.nn.sigmoid(x)


ROW_PITCH = 2


def _shift_buffer(rows, d):
    return pltpu.VMEM((d // LANES, ROW_PITCH * rows, LANES), jnp.float32)


def _shift_rows(row0, n):
    return pl.ds(ROW_PITCH * row0, n, stride=ROW_PITCH)


def _shift_store(buf, row0, val):
    for j in range(val.shape[1] // LANES):
        buf[j, _shift_rows(row0, val.shape[0]), :] = val[:, j * LANES:(j + 1) * LANES]


def _shift_zero_halo(buf, halo):
    for j in range(buf.shape[0]):
        buf[j, _shift_rows(0, halo), :] = jnp.zeros((halo, LANES), jnp.float32)


def _shift_carry_halo(buf, halo, tm):
    for j in range(buf.shape[0]):
        buf[j, _shift_rows(0, halo), :] = buf[j, _shift_rows(tm, halo), :]


def _ada_kernel(c_ref, w_ref, b_ref, o_ref):
    ca = _silu(c_ref[...])
    o_ref[0] = jnp.dot(ca, w_ref[0], preferred_element_type=jnp.float32) + b_ref[0]


def _ada_mod(c, ada_w, ada_b):
    depth, d, n = ada_w.shape
    bsz = c.shape[0]
    rows = -(-bsz // SUBLANES) * SUBLANES
    c_pad = jnp.zeros((rows, d), c.dtype).at[:bsz].set(c)
    out = pl.pallas_call(
        _ada_kernel,
        grid=(depth, n // ADA_TILE),
        in_specs=[
            pl.BlockSpec((rows, d), lambda i, j: (0, 0)),
            pl.BlockSpec((1, d, ADA_TILE), lambda i, j: (i, 0, j)),
            pl.BlockSpec((1, 1, ADA_TILE), lambda i, j: (i, 0, j)),
        ],
        out_specs=pl.BlockSpec((1, rows, ADA_TILE), lambda i, j: (i, 0, j)),
        out_shape=jax.ShapeDtypeStruct((depth, rows, n), jnp.float32),
        compiler_params=pltpu.CompilerParams(
            dimension_semantics=("arbitrary", "arbitrary"), vmem_limit_bytes=VMEM_LIMIT),
        name="ada_mod",
    )(c_pad, ada_w, ada_b.reshape(depth, 1, n))
    return out[:, :bsz].reshape(depth, bsz, N_ADA, d)


def _zero_of(anchor):
    bits = pltpu.bitcast(anchor, jnp.uint32)
    return pltpu.bitcast((bits >> 16) >> 16, jnp.float32)


def _anchor(val):
    return val[0:SUBLANES, 0:LANES]


def _operand_after(ref, anchor):
    x = ref[...]
    if anchor is None:
        return x
    zero = _zero_of(anchor)
    top = (x[0:2 * SUBLANES, 0:LANES].astype(jnp.float32)
           + jnp.concatenate([zero, zero], axis=0)).astype(ref.dtype)
    top_rows = jnp.concatenate([top, x[0:2 * SUBLANES, LANES:]], axis=1)
    return jnp.concatenate([top_rows, x[2 * SUBLANES:]], axis=0)


def _deal(items, n):
    return [items[len(items) * k // n:len(items) * (k + 1) // n] for k in range(n)]


def _block_cols(blocks):
    return slice(blocks[0] * MXU_COLS, (blocks[-1] + 1) * MXU_COLS)


def _after(x, anchor):
    reps = (x.shape[0] // SUBLANES, x.shape[1] // LANES)
    return x + jnp.tile(_zero_of(anchor), reps)


def _first_step_init(n, *bufs):
    @pl.when(n == 0)
    def _():
        for buf in bufs:
            buf[...] = jnp.zeros(buf.shape, buf.dtype)


def _mixer_residual(xlag_ref, m, y, gf_ref, o_ref, hb):
    xm = xlag_ref[0] + (1.0 + m[2:3]) * y
    o_ref[0] = xm
    hb[...] = _rms_mod(xm, gf_ref[...], m[3:4], m[4:5]).astype(jnp.bfloat16)


def _ffn_up_blocks(hb, wg_ref, wu_ref, act, blocks, after):
    cols = _block_cols(blocks)
    n = cols.stop - cols.start
    w = jnp.concatenate([wg_ref[:, cols], wu_ref[:, cols]], axis=1)
    ab = jnp.dot(_operand_after(hb, after), w, preferred_element_type=jnp.float32)
    prod = _silu(ab[:, :n]) * ab[:, n:]
    act[:, cols] = prod.astype(jnp.bfloat16)
    return _anchor(prod)


def _ffn_down_blocks(act, m, wd_ref, o_ref, blocks, after):
    cols = _block_cols(blocks)
    y = jnp.dot(_operand_after(act, after), wd_ref[:, cols], preferred_element_type=jnp.float32)
    o_ref[0, :, cols] = o_ref[0, :, cols] + (1.0 + m[5:6, cols]) * y
    return _anchor(y)


def _final_norm(o_ref, fg_ref):
    out = o_ref[0]
    ms = jnp.mean(out * out, axis=-1, keepdims=True)
    o_ref[0] = out * lax.rsqrt(ms + EPS) * fg_ref[...]


def _conv_units(units, wdw_ref, bdw_ref, ubuf, ybuf, after):
    taps = {}
    for j, i in units:
        lanes = slice(j * LANES, (j + 1) * LANES)
        if j not in taps:
            w = wdw_ref[:, lanes]
            taps = {j: [jnp.broadcast_to(w[k:k + 1, :], (UNIT_ROWS, LANES))
                        for k in range(CONV_WIDTH)]}
        acc = jnp.tile(_zero_of(after), (UNIT_ROWS // SUBLANES, 1))
        for k in range(CONV_WIDTH):
            start = CONV_HALO + i * UNIT_ROWS - (CONV_WIDTH - 1) + k
            acc = acc + ubuf[j, _shift_rows(start, UNIT_ROWS), :] * taps[j][k]
        ybuf[i * UNIT_ROWS:(i + 1) * UNIT_ROWS, lanes] = acc + bdw_ref[:, lanes]
        after = _anchor(acc)
    return after


def _conv_layer_kernel(x_ref, xlag_ref, mod_ref, modlag_ref, gm_ref, w1_ref, b1_ref, wdw_ref,
                       bdw_ref, lng_ref, lnb_ref, w2_ref, b2_ref, gf_ref, wg_ref, wu_ref, wd_ref,
                       fg_ref, o_ref, hb, act, ha, gbuf, ubuf, ybuf, *, tiles_per_seq, final):
    n = pl.program_id(0)
    tm, d = x_ref.shape[1], x_ref.shape[2]
    f = act.shape[1]
    up_groups = _deal(list(range(f // MXU_COLS)), CONV_LAYER_SPLITS["up"])
    down_groups = _deal(list(range(d // MXU_COLS)), CONV_LAYER_SPLITS["down"])
    head_groups = _deal(list(range(d // MXU_COLS)), CONV_LAYER_SPLITS["head"])
    units = [(j, i) for j in range(d // LANES) for i in range(tm // UNIT_ROWS)]
    _first_step_init(n, gbuf, ybuf)

    @pl.when((n == 0) | (lax.rem(n + tiles_per_seq - 1, tiles_per_seq) == 0))
    def _():
        _shift_zero_halo(ubuf, CONV_HALO)

    def head_norm(after):
        m = mod_ref[0]
        x = x_ref[0] if after is None else _after(x_ref[0], after)
        ha[...] = _rms_mod(x, gm_ref[...], m[0:1], m[1:2]).astype(jnp.bfloat16)

    def head_matmul(blocks, after):
        cols = _block_cols(blocks)
        gate_cols = slice(d + cols.start, d + cols.stop)
        nc = cols.stop - cols.start
        w = jnp.concatenate([w1_ref[:, cols], w1_ref[:, gate_cols]], axis=1)
        b = jnp.concatenate([b1_ref[:, cols], b1_ref[:, gate_cols]], axis=1)
        u = jnp.dot(_operand_after(ha, after), w, preferred_element_type=jnp.float32) + b
        gbuf[:, cols] = u[:, :nc] * jax.nn.sigmoid(u[:, nc:])
        return _anchor(u)

    _conv_layer_step(xlag_ref, modlag_ref, wdw_ref, bdw_ref, lng_ref, lnb_ref, w2_ref,
                     b2_ref, gf_ref, wg_ref, wu_ref, wd_ref, fg_ref, o_ref, hb, act, gbuf, ubuf,
                     ybuf, head_norm, head_matmul, units, up_groups, down_groups, head_groups,
                     final)


def _conv_layer_step(xlag_ref, modlag_ref, wdw_ref, bdw_ref, lng_ref, lnb_ref, w2_ref, b2_ref,
                     gf_ref, wg_ref, wu_ref, wd_ref, fg_ref, o_ref, hb, act, gbuf, ubuf, ybuf,
                     head_norm, head_matmul, units, up_groups, down_groups, head_groups, final):
    tm = xlag_ref.shape[1]
    n_matmuls = len(up_groups) + len(down_groups) + len(head_groups)
    unit_groups = iter(_deal(units, n_matmuls - 1))
    _shift_store(ubuf, CONV_HALO, gbuf[...])
    v = ybuf[...]
    mu = jnp.mean(v, axis=-1, keepdims=True)
    vc = v - mu
    var = jnp.mean(vc * vc, axis=-1, keepdims=True)
    v = _silu(vc * lax.rsqrt(var + EPS) * lng_ref[...] + lnb_ref[...])
    y = jnp.dot(v.astype(jnp.bfloat16), w2_ref[...],
                preferred_element_type=jnp.float32) + b2_ref[...]
    _mixer_residual(xlag_ref, modlag_ref[0], y, gf_ref, o_ref, hb)

    done, ready = _anchor(y), None
    for k, blocks in enumerate(up_groups):
        done_next = _ffn_up_blocks(hb, wg_ref, wu_ref, act, blocks, ready)
        ready = _conv_units(next(unit_groups), wdw_ref, bdw_ref, ubuf, ybuf, done)
        if k == 0:
            head_norm(ready)
        done = done_next
    for blocks in down_groups:
        done_next = _ffn_down_blocks(act, modlag_ref[0], wd_ref, o_ref, blocks, ready)
        ready = _conv_units(next(unit_groups), wdw_ref, bdw_ref, ubuf, ybuf, done)
        done = done_next
    for k, blocks in enumerate(head_groups):
        done_next = head_matmul(blocks, ready)
        if k + 1 < len(head_groups):
            ready = _conv_units(next(unit_groups), wdw_ref, bdw_ref, ubuf, ybuf, done)
            done = done_next
    _shift_carry_halo(ubuf, CONV_HALO, tm)
    if final:
        _final_norm(o_ref, fg_ref)


def _pool_units(chunks, seq_row0, hbuf, mixbuf, after):
    slabs_per_group = hbuf.shape[0] // len(POOL_WINDOWS)
    for i in chunks:
        row0 = i * UNIT_ROWS
        zero = jnp.tile(_zero_of(after), (UNIT_ROWS // SUBLANES, 1))
        t = seq_row0 + row0 + lax.broadcasted_iota(jnp.int32, (UNIT_ROWS, LANES), 0)
        for gi, win in enumerate(POOL_WINDOWS):
            cnt = jnp.minimum(t + 1, win).astype(jnp.float32)
            for sl in range(gi * slabs_per_group, (gi + 1) * slabs_per_group):
                h = hbuf[sl, _shift_rows(POOL_HALO + row0, UNIT_ROWS), :]
                wsum = h + zero
                for j in range(1, win):
                    wsum = wsum + hbuf[sl, _shift_rows(POOL_HALO + row0 - j, UNIT_ROWS), :]
                mixbuf[row0:row0 + UNIT_ROWS, sl * LANES:(sl + 1) * LANES] = (
                    wsum / cnt - h).astype(jnp.bfloat16)
        after = _anchor(wsum)
    return after


def _pool_layer_kernel(x_ref, xlag_ref, mod_ref, modlag_ref, gm_ref, pw_ref, ls_ref, gf_ref,
                       wg_ref, wu_ref, wd_ref, fg_ref, o_ref, hb, act, hplain, hbuf, mixbuf, *,
                       tiles_per_seq, final):
    n = pl.program_id(0)
    tm, d = x_ref.shape[1], x_ref.shape[2]
    f = act.shape[1]
    window_tile = lax.rem(n + tiles_per_seq - 1, tiles_per_seq)
    up_groups = _deal(list(range(f // MXU_COLS)), POOL_LAYER_SPLITS["up"])
    down_groups = _deal(list(range(d // MXU_COLS)), POOL_LAYER_SPLITS["down"])
    chunks = list(range(tm // UNIT_ROWS))
    _first_step_init(n, hplain)

    @pl.when((n == 0) | (window_tile == 0))
    def _():
        _shift_zero_halo(hbuf, POOL_HALO)

    def head_norm(after):
        m = mod_ref[0]
        h = _rms_mod(_after(x_ref[0], after), gm_ref[...], m[0:1], m[1:2])
        hplain[...] = h
        return _anchor(h)

    @pl.when(n < LAG)
    def _():
        _shift_store(hbuf, POOL_HALO, hplain[...])
        done = _pool_units(chunks, window_tile * tm, hbuf, mixbuf, _anchor(x_ref[0]))
        _shift_carry_halo(hbuf, POOL_HALO, tm)
        head_norm(done)

    pl.when(n >= LAG)(functools.partial(
        _pool_layer_step, xlag_ref, modlag_ref, pw_ref, ls_ref, gf_ref, wg_ref, wu_ref, wd_ref,
        fg_ref, o_ref, hb, act, hplain, hbuf, mixbuf, head_norm, chunks, up_groups, down_groups,
        window_tile * tm, final))


def _pool_layer_step(xlag_ref, modlag_ref, pw_ref, ls_ref, gf_ref, wg_ref, wu_ref, wd_ref, fg_ref,
                     o_ref, hb, act, hplain, hbuf, mixbuf, head_norm, chunks, up_groups,
                     down_groups, seq_row0, final):
    tm, d = xlag_ref.shape[1], xlag_ref.shape[2]
    dg = d // len(POOL_WINDOWS)
    chunk_groups = _deal(chunks, len(up_groups) + len(down_groups) - 1)
    _shift_store(hbuf, POOL_HALO, hplain[...])
    ys = [jnp.dot(mixbuf[:, gi * dg:(gi + 1) * dg], pw_ref[gi],
                  preferred_element_type=jnp.float32) for gi in range(len(POOL_WINDOWS))]
    y = jnp.concatenate(ys, axis=1) * ls_ref[...]
    _mixer_residual(xlag_ref, modlag_ref[0], y, gf_ref, o_ref, hb)

    done, ready = _anchor(y), None
    matmuls = ([functools.partial(_ffn_up_blocks, hb, wg_ref, wu_ref, act, blocks)
                for blocks in up_groups]
               + [functools.partial(_ffn_down_blocks, act, modlag_ref[0], wd_ref, o_ref, blocks)
                  for blocks in down_groups])
    for k, matmul in enumerate(matmuls):
        done_next = matmul(ready)
        if k < len(chunk_groups):
            ready = _pool_units(chunk_groups[k], seq_row0, hbuf, mixbuf, done)
            done = done_next
        if k == len(chunk_groups) - 1:
            ready = head_norm(ready)
    _shift_carry_halo(hbuf, POOL_HALO, tm)
    if final:
        _final_norm(o_ref, fg_ref)


def _layer_call(body, name, x, mod, consts, scratch, final):
    bsz, seq, d = x.shape
    tm = SEQ_TILE
    tiles_per_seq = seq // tm
    n_tiles = bsz * tiles_per_seq
    head_tile = lambda n: jnp.minimum(n, n_tiles - 1)
    tail_tile = lambda n: jnp.maximum(n - LAG, 0)
    xt = x.reshape(n_tiles, tm, d)
    out = pl.pallas_call(
        functools.partial(body, tiles_per_seq=tiles_per_seq, final=final),
        grid=(n_tiles + LAG,),
        in_specs=[
            pl.BlockSpec((1, tm, d), lambda n: (head_tile(n), 0, 0)),
            pl.BlockSpec((1, tm, d), lambda n: (tail_tile(n), 0, 0)),
            pl.BlockSpec((1, N_ADA, d), lambda n: (head_tile(n) // tiles_per_seq, 0, 0)),
            pl.BlockSpec((1, N_ADA, d), lambda n: (tail_tile(n) // tiles_per_seq, 0, 0)),
        ] + [_const_spec(a, layer) for a, layer in consts],
        out_specs=pl.BlockSpec((1, tm, d), lambda n: (tail_tile(n), 0, 0)),
        out_shape=jax.ShapeDtypeStruct((n_tiles, tm, d), x.dtype),
        scratch_shapes=list(scratch),
        compiler_params=pltpu.CompilerParams(
            dimension_semantics=("arbitrary",), vmem_limit_bytes=VMEM_LIMIT),
        name=name,
    )(xt, xt, mod, mod, *[a for a, _ in consts])
    return out.reshape(bsz, seq, d)


def kernel(x, c, ada_w, ada_b, norm_mix_g, norm_ffn_g, conv_w1, conv_b1, conv_wdw, conv_bdw, conv_ln_g, conv_ln_b, conv_w2, conv_b2, pool_w, pool_ls, ffn_w_gate, ffn_w_up, ffn_w_down, final_g):
    depth = ada_w.shape[0]
    d = x.shape[-1]
    f = ffn_w_gate.shape[-1]
    whole = lambda a: (a.reshape(1, -1), None)
    bf16 = lambda a: a.astype(jnp.bfloat16)
    wg, wu, wd = bf16(ffn_w_gate), bf16(ffn_w_up), bf16(ffn_w_down)
    w1, w2, pw = bf16(conv_w1), bf16(conv_w2), bf16(pool_w)
    mod = _ada_mod(c, ada_w, ada_b)
    ffn_scratch = (pltpu.VMEM((SEQ_TILE, d), jnp.bfloat16), pltpu.VMEM((SEQ_TILE, f), jnp.bfloat16))
    for i in range(depth):
        j = i // 2
        final = i == depth - 1
        ffn_consts = (whole(norm_ffn_g[i]), (wg, i), (wu, i), (wd, i), whole(final_g))
        if i % 2 == 0:
            consts = (whole(norm_mix_g[i]), (w1, j), whole(conv_b1[j]), (conv_wdw, j),
                      whole(conv_bdw[j]), whole(conv_ln_g[j]), whole(conv_ln_b[j]),
                      (w2, j), whole(conv_b2[j])) + ffn_consts
            scratch = ffn_scratch + (pltpu.VMEM((SEQ_TILE, d), jnp.bfloat16),
                                     pltpu.VMEM((SEQ_TILE, d), jnp.float32),
                                     _shift_buffer(CONV_HALO + SEQ_TILE, d),
                                     pltpu.VMEM((SEQ_TILE, d), jnp.float32))
            x = _layer_call(_conv_layer_kernel, "conv_layer", x, mod[i], consts, scratch, final)
        else:
            consts = (whole(norm_mix_g[i]), (pw, j), whole(pool_ls[j])) + ffn_consts
            scratch = ffn_scratch + (pltpu.VMEM((SEQ_TILE, d), jnp.float32),
                                     _shift_buffer(POOL_HALO + SEQ_TILE, d),
                                     pltpu.VMEM((SEQ_TILE, d), jnp.bfloat16))
            x = _layer_call(_pool_layer_kernel, "pool_layer", x, mod[i], consts, scratch, final)
    return x
```

```python
import functools

import jax
import jax.numpy as jnp
from jax import lax
from jax.experimental import pallas as pl
from jax.experimental.pallas import tpu as pltpu

EPS = 1e-6
N_ADA = 6
CONV_WIDTH = 31
POOL_WINDOWS = (2, 4, 8, 16)

LANES = 128
SUBLANES = 8
MXU_COLS = 256
CONV_HALO = 32
POOL_HALO = 16
UNIT_ROWS = 64
SEQ_TILE = 512
LAG = 2
ADA_TILE = 1536
CONV_LAYER_SPLITS = dict(up=4, down=2, head=2)
POOL_LAYER_SPLITS = dict(up=2, down=1)
VMEM_LIMIT = 62 * 1024 * 1024


def _const_spec(a, layer=None):
    if layer is None:
        zeros = (0,) * a.ndim
        return pl.BlockSpec(a.shape, lambda *_: zeros, pipeline_mode=pl.Buffered(1))
    index = (layer,) + (0,) * (a.ndim - 1)
    return pl.BlockSpec((None,) + a.shape[1:], lambda *_: index, pipeline_mode=pl.Buffered(1))


def _rms_mod(x, g, shift, scale):
    ms = jnp.mean(x * x, axis=-1, keepdims=True)
    y = x * lax.rsqrt(ms + EPS) * g
    return y * (1.0 + scale) + shift


def _silu(x):
    return x * jax.nn.sigmoid(x)


ROW_PITCH = 2


def _shift_buffer(rows, d):
    return pltpu.VMEM((d // LANES, ROW_PITCH * rows, LANES), jnp.float32)


def _shift_rows(row0, n):
    return pl.ds(ROW_PITCH * row0, n, stride=ROW_PITCH)


def _shift_store(buf, row0, val):
    for j in range(val.shape[1] // LANES):
        buf[j, _shift_rows(row0, val.shape[0]), :] = val[:, j * LANES:(j + 1) * LANES]


def _shift_zero_halo(buf, halo):
    for j in range(buf.shape[0]):
        buf[j, _shift_rows(0, halo), :] = jnp.zeros((halo, LANES), jnp.float32)


def _shift_carry_halo(buf, halo, tm):
    for j in range(buf.shape[0]):
        buf[j, _shift_rows(0, halo), :] = buf[j, _shift_rows(tm, halo), :]


def _ada_kernel(c_ref, w_ref, b_ref, o_ref):
    ca = _silu(c_ref[...])
    o_ref[0] = jnp.dot(ca, w_ref[0], preferred_element_type=jnp.float32) + b_ref[0]


def _ada_mod(c, ada_w, ada_b):
    depth, d, n = ada_w.shape
    bsz = c.shape[0]
    rows = -(-bsz // SUBLANES) * SUBLANES
    c_pad = jnp.zeros((rows, d), c.dtype).at[:bsz].set(c)
    out = pl.pallas_call(
        _ada_kernel,
        grid=(depth, n // ADA_TILE),
        in_specs=[
            pl.BlockSpec((rows, d), lambda i, j: (0, 0)),
            pl.BlockSpec((1, d, ADA_TILE), lambda i, j: (i, 0, j)),
            pl.BlockSpec((1, 1, ADA_TILE), lambda i, j: (i, 0, j)),
        ],
        out_specs=pl.BlockSpec((1, rows, ADA_TILE), lambda i, j: (i, 0, j)),
        out_shape=jax.ShapeDtypeStruct((depth, rows, n), jnp.float32),
        compiler_params=pltpu.CompilerParams(
            dimension_semantics=("arbitrary", "arbitrary"), vmem_limit_bytes=VMEM_LIMIT),
        name="ada_mod",
    )(c_pad, ada_w, ada_b.reshape(depth, 1, n))
    return out[:, :bsz].reshape(depth, bsz, N_ADA, d)


def _zero_of(anchor):
    bits = pltpu.bitcast(anchor, jnp.uint32)
    return pltpu.bitcast((bits >> 16) >> 16, jnp.float32)


def _anchor(val):
    return val[0:SUBLANES, 0:LANES]


def _operand_after(ref, anchor):
    x = ref[...]
    if anchor is None:
        return x
    zero = _zero_of(anchor)
    top = (x[0:2 * SUBLANES, 0:LANES].astype(jnp.float32)
           + jnp.concatenate([zero, zero], axis=0)).astype(ref.dtype)
    top_rows = jnp.concatenate([top, x[0:2 * SUBLANES, LANES:]], axis=1)
    return jnp.concatenate([top_rows, x[2 * SUBLANES:]], axis=0)


def _deal(items, n):
    return [items[len(items) * k // n:len(items) * (k + 1) // n] for k in range(n)]


def _block_cols(blocks):
    return slice(blocks[0] * MXU_COLS, (blocks[-1] + 1) * MXU_COLS)


def _after(x, anchor):
    reps = (x.shape[0] // SUBLANES, x.shape[1] // LANES)
    return x + jnp.tile(_zero_of(anchor), reps)


def _first_step_init(n, *bufs):
    @pl.when(n == 0)
    def _():
        for buf in bufs:
            buf[...] = jnp.zeros(buf.shape, buf.dtype)


def _mixer_residual(xlag_ref, m, y, gf_ref, o_ref, hb):
    xm = xlag_ref[0] + (1.0 + m[2:3]) * y
    o_ref[0] = xm
    hb[...] = _rms_mod(xm, gf_ref[...], m[3:4], m[4:5]).astype(jnp.bfloat16)


def _ffn_up_blocks(hb, wg_ref, wu_ref, act, blocks, after):
    cols = _block_cols(blocks)
    n = cols.stop - cols.start
    w = jnp.concatenate([wg_ref[:, cols], wu_ref[:, cols]], axis=1)
    ab = jnp.dot(_operand_after(hb, after), w, preferred_element_type=jnp.float32)
    prod = _silu(ab[:, :n]) * ab[:, n:]
    act[:, cols] = prod.astype(jnp.bfloat16)
    return _anchor(prod)


def _ffn_down_blocks(act, m, wd_ref, o_ref, blocks, after):
    cols = _block_cols(blocks)
    y = jnp.dot(_operand_after(act, after), wd_ref[:, cols], preferred_element_type=jnp.float32)
    o_ref[0, :, cols] = o_ref[0, :, cols] + (1.0 + m[5:6, cols]) * y
    return _anchor(y)


def _final_norm(o_ref, fg_ref):
    out = o_ref[0]
    ms = jnp.mean(out * out, axis=-1, keepdims=True)
    o_ref[0] = out * lax.rsqrt(ms + EPS) * fg_ref[...]


def _conv_units(units, wdw_ref, bdw_ref, ubuf, ybuf, after):
    taps = {}
    for j, i in units:
        lanes = slice(j * LANES, (j + 1) * LANES)
        if j not in taps:
            w = wdw_ref[:, lanes]
            taps = {j: [jnp.broadcast_to(w[k:k + 1, :], (UNIT_ROWS, LANES))
                        for k in range(CONV_WIDTH)]}
        acc = jnp.tile(_zero_of(after), (UNIT_ROWS // SUBLANES, 1))
        for k in range(CONV_WIDTH):
            start = CONV_HALO + i * UNIT_ROWS - (CONV_WIDTH - 1) + k
            acc = acc + ubuf[j, _shift_rows(start, UNIT_ROWS), :] * taps[j][k]
        ybuf[i * UNIT_ROWS:(i + 1) * UNIT_ROWS, lanes] = acc + bdw_ref[:, lanes]
        after = _anchor(acc)
    return after


def _conv_layer_kernel(x_ref, xlag_ref, mod_ref, modlag_ref, gm_ref, w1_ref, b1_ref, wdw_ref,
                       bdw_ref, lng_ref, lnb_ref, w2_ref, b2_ref, gf_ref, wg_ref, wu_ref, wd_ref,
                       fg_ref, o_ref, hb, act, ha, gbuf, ubuf, ybuf, *, tiles_per_seq, final):
    n = pl.program_id(0)
    tm, d = x_ref.shape[1], x_ref.shape[2]
    f = act.shape[1]
    up_groups = _deal(list(range(f // MXU_COLS)), CONV_LAYER_SPLITS["up"])
    down_groups = _deal(list(range(d // MXU_COLS)), CONV_LAYER_SPLITS["down"])
    head_groups = _deal(list(range(d // MXU_COLS)), CONV_LAYER_SPLITS["head"])
    units = [(j, i) for j in range(d // LANES) for i in range(tm // UNIT_ROWS)]
    _first_step_init(n, gbuf, ybuf)

    @pl.when((n == 0) | (lax.rem(n + tiles_per_seq - 1, tiles_per_seq) == 0))
    def _():
        _shift_zero_halo(ubuf, CONV_HALO)

    def head_norm(after):
        m = mod_ref[0]
        ha[...] = _rms_mod(_after(x_ref[0], after), gm_ref[...], m[0:1],
                           m[1:2]).astype(jnp.bfloat16)

    @pl.when(n == 0)
    def _():
        m = modlag_ref[0]
        ha[...] = _rms_mod(xlag_ref[0], gm_ref[...], m[0:1], m[1:2]).astype(jnp.bfloat16)

    def head_matmul(blocks, after):
        cols = _block_cols(blocks)
        gate_cols = slice(d + cols.start, d + cols.stop)
        nc = cols.stop - cols.start
        w = jnp.concatenate([w1_ref[:, cols], w1_ref[:, gate_cols]], axis=1)
        b = jnp.concatenate([b1_ref[:, cols], b1_ref[:, gate_cols]], axis=1)
        u = jnp.dot(_operand_after(ha, after), w, preferred_element_type=jnp.float32) + b
        gbuf[:, cols] = u[:, :nc] * jax.nn.sigmoid(u[:, nc:])
        return _anchor(u)

    _conv_layer_step(xlag_ref, modlag_ref, wdw_ref, bdw_ref, lng_ref, lnb_ref, w2_ref,
                     b2_ref, gf_ref, wg_ref, wu_ref, wd_ref, fg_ref, o_ref, hb, act, gbuf, ubuf,
                     ybuf, head_norm, head_matmul, units, up_groups, down_groups, head_groups,
                     final)


def _conv_layer_step(xlag_ref, modlag_ref, wdw_ref, bdw_ref, lng_ref, lnb_ref, w2_ref, b2_ref,
                     gf_ref, wg_ref, wu_ref, wd_ref, fg_ref, o_ref, hb, act, gbuf, ubuf, ybuf,
                     head_norm, head_matmul, units, up_groups, down_groups, head_groups, final):
    tm = xlag_ref.shape[1]
    unit_groups = iter(_deal(units, len(up_groups) + len(down_groups) - 1))
    _shift_store(ubuf, CONV_HALO, gbuf[...])
    head_matmul(head_groups[0], None)
    v = ybuf[...]
    mu = jnp.mean(v, axis=-1, keepdims=True)
    vc = v - mu
    var = jnp.mean(vc * vc, axis=-1, keepdims=True)
    v = _silu(vc * lax.rsqrt(var + EPS) * lng_ref[...] + lnb_ref[...])
    y = jnp.dot(v.astype(jnp.bfloat16), w2_ref[...],
                preferred_element_type=jnp.float32) + b2_ref[...]
    for blocks in head_groups[1:]:
        head_matmul(blocks, None)
    _mixer_residual(xlag_ref, modlag_ref[0], y, gf_ref, o_ref, hb)

    done, ready = _anchor(y), None
    for k, blocks in enumerate(up_groups):
        done_next = _ffn_up_blocks(hb, wg_ref, wu_ref, act, blocks, ready)
        ready = _conv_units(next(unit_groups), wdw_ref, bdw_ref, ubuf, ybuf, done)
        if k == 0:
            head_norm(ready)
        done = done_next
    for k, blocks in enumerate(down_groups):
        done_next = _ffn_down_blocks(act, modlag_ref[0], wd_ref, o_ref, blocks, ready)
        if k + 1 < len(down_groups):
            ready = _conv_units(next(unit_groups), wdw_ref, bdw_ref, ubuf, ybuf, done)
            done = done_next
    _shift_carry_halo(ubuf, CONV_HALO, tm)
    if final:
        _final_norm(o_ref, fg_ref)


def _pool_units(chunks, seq_row0, hbuf, mixbuf, after):
    slabs_per_group = hbuf.shape[0] // len(POOL_WINDOWS)
    for i in chunks:
        row0 = i * UNIT_ROWS
        zero = jnp.tile(_zero_of(after), (UNIT_ROWS // SUBLANES, 1))
        t = seq_row0 + row0 + lax.broadcasted_iota(jnp.int32, (UNIT_ROWS, LANES), 0)
        for gi, win in enumerate(POOL_WINDOWS):
            cnt = jnp.minimum(t + 1, win).astype(jnp.float32)
            for sl in range(gi * slabs_per_group, (gi + 1) * slabs_per_group):
                h = hbuf[sl, _shift_rows(POOL_HALO + row0, UNIT_ROWS), :]
                wsum = h + zero
                for j in range(1, win):
                    wsum = wsum + hbuf[sl, _shift_rows(POOL_HALO + row0 - j, UNIT_ROWS), :]
                mixbuf[row0:row0 + UNIT_ROWS, sl * LANES:(sl + 1) * LANES] = (
                    wsum / cnt - h).astype(jnp.bfloat16)
        after = _anchor(wsum)
    return after


def _pool_layer_kernel(x_ref, xlag_ref, mod_ref, modlag_ref, gm_ref, pw_ref, ls_ref, gf_ref,
                       wg_ref, wu_ref, wd_ref, fg_ref, o_ref, hb, act, hplain, hbuf, mixbuf, *,
                       tiles_per_seq, final):
    n = pl.program_id(0)
    tm, d = x_ref.shape[1], x_ref.shape[2]
    f = act.shape[1]
    window_tile = lax.rem(n + tiles_per_seq - 1, tiles_per_seq)
    up_groups = _deal(list(range(f // MXU_COLS)), POOL_LAYER_SPLITS["up"])
    down_groups = _deal(list(range(d // MXU_COLS)), POOL_LAYER_SPLITS["down"])
    chunks = list(range(tm // UNIT_ROWS))
    _first_step_init(n, hplain)

    @pl.when((n == 0) | (window_tile == 0))
    def _():
        _shift_zero_halo(hbuf, POOL_HALO)

    def head_norm(after):
        m = mod_ref[0]
        h = _rms_mod(_after(x_ref[0], after), gm_ref[...], m[0:1], m[1:2])
        hplain[...] = h
        return _anchor(h)

    @pl.when(n < LAG)
    def _():
        _shift_store(hbuf, POOL_HALO, hplain[...])
        done = _pool_units(chunks, window_tile * tm, hbuf, mixbuf, _anchor(x_ref[0]))
        _shift_carry_halo(hbuf, POOL_HALO, tm)
        head_norm(done)

    pl.when(n >= LAG)(functools.partial(
        _pool_layer_step, xlag_ref, modlag_ref, pw_ref, ls_ref, gf_ref, wg_ref, wu_ref, wd_ref,
        fg_ref, o_ref, hb, act, hplain, hbuf, mixbuf, head_norm, chunks, up_groups, down_groups,
        window_tile * tm, final))


def _pool_layer_step(xlag_ref, modlag_ref, pw_ref, ls_ref, gf_ref, wg_ref, wu_ref, wd_ref, fg_ref,
                     o_ref, hb, act, hplain, hbuf, mixbuf, head_norm, chunks, up_groups,
                     down_groups, seq_row0, final):
    tm, d = xlag_ref.shape[1], xlag_ref.shape[2]
    dg = d // len(POOL_WINDOWS)
    chunk_groups = _deal(chunks, len(up_groups) + len(down_groups) - 1)
    _shift_store(hbuf, POOL_HALO, hplain[...])
    ys = [jnp.dot(mixbuf[:, gi * dg:(gi + 1) * dg], pw_ref[gi],
                  preferred_element_type=jnp.float32) for gi in range(len(POOL_WINDOWS))]
    y = jnp.concatenate(ys, axis=1) * ls_ref[...]
    _mixer_residual(xlag_ref, modlag_ref[0], y, gf_ref, o_ref, hb)

    done, ready = _anchor(y), None
    matmuls = ([functools.partial(_ffn_up_blocks, hb, wg_ref, wu_ref, act, blocks)
                for blocks in up_groups]
               + [functools.partial(_ffn_down_blocks, act, modlag_ref[0], wd_ref, o_ref, blocks)
                  for blocks in down_groups])
    for k, matmul in enumerate(matmuls):
        done_next = matmul(ready)
        if k < len(chunk_groups):
            ready = _pool_units(chunk_groups[k], seq_row0, hbuf, mixbuf, done)
            done = done_next
        if k == len(chunk_groups) - 1:
            ready = head_norm(ready)
    _shift_carry_halo(hbuf, POOL_HALO, tm)
    if final:
        _final_norm(o_ref, fg_ref)


def _layer_call(body, name, x, mod, consts, scratch, final, head_ahead):
    bsz, seq, d = x.shape
    tm = SEQ_TILE
    tiles_per_seq = seq // tm
    n_tiles = bsz * tiles_per_seq
    head_tile = lambda n: jnp.minimum(n + head_ahead, n_tiles - 1)
    tail_tile = lambda n: jnp.maximum(n - LAG, 0)
    xt = x.reshape(n_tiles, tm, d)
    out = pl.pallas_call(
        functools.partial(body, tiles_per_seq=tiles_per_seq, final=final),
        grid=(n_tiles + LAG,),
        in_specs=[
            pl.BlockSpec((1, tm, d), lambda n: (head_tile(n), 0, 0)),
            pl.BlockSpec((1, tm, d), lambda n: (tail_tile(n), 0, 0)),
            pl.BlockSpec((1, N_ADA, d), lambda n: (head_tile(n) // tiles_per_seq, 0, 0)),
            pl.BlockSpec((1, N_ADA, d), lambda n: (tail_tile(n) // tiles_per_seq, 0, 0)),
        ] + [_const_spec(a, layer) for a, layer in consts],
        out_specs=pl.BlockSpec((1, tm, d), lambda n: (tail_tile(n), 0, 0)),
        out_shape=jax.ShapeDtypeStruct((n_tiles, tm, d), x.dtype),
        scratch_shapes=list(scratch),
        compiler_params=pltpu.CompilerParams(
            dimension_semantics=("arbitrary",), vmem_limit_bytes=VMEM_LIMIT),
        name=name,
    )(xt, xt, mod, mod, *[a for a, _ in consts])
    return out.reshape(bsz, seq, d)


def kernel(x, c, ada_w, ada_b, norm_mix_g, norm_ffn_g, conv_w1, conv_b1, conv_wdw, conv_bdw, conv_ln_g, conv_ln_b, conv_w2, conv_b2, pool_w, pool_ls, ffn_w_gate, ffn_w_up, ffn_w_down, final_g):
    depth = ada_w.shape[0]
    d = x.shape[-1]
    f = ffn_w_gate.shape[-1]
    whole = lambda a: (a.reshape(1, -1), None)
    bf16 = lambda a: a.astype(jnp.bfloat16)
    wg, wu, wd = bf16(ffn_w_gate), bf16(ffn_w_up), bf16(ffn_w_down)
    w1, w2, pw = bf16(conv_w1), bf16(conv_w2), bf16(pool_w)
    mod = _ada_mod(c, ada_w, ada_b)
    ffn_scratch = (pltpu.VMEM((SEQ_TILE, d), jnp.bfloat16), pltpu.VMEM((SEQ_TILE, f), jnp.bfloat16))
    for i in range(depth):
        j = i // 2
        final = i == depth - 1
        ffn_consts = (whole(norm_ffn_g[i]), (wg, i), (wu, i), (wd, i), whole(final_g))
        if i % 2 == 0:
            consts = (whole(norm_mix_g[i]), (w1, j), whole(conv_b1[j]), (conv_wdw, j),
                      whole(conv_bdw[j]), whole(conv_ln_g[j]), whole(conv_ln_b[j]),
                      (w2, j), whole(conv_b2[j])) + ffn_consts
            scratch = ffn_scratch + (pltpu.VMEM((SEQ_TILE, d), jnp.bfloat16),
                                     pltpu.VMEM((SEQ_TILE, d), jnp.float32),
                                     _shift_buffer(CONV_HALO + SEQ_TILE, d),
                                     pltpu.VMEM((SEQ_TILE, d), jnp.float32))
            x = _layer_call(_conv_layer_kernel, "conv_layer", x, mod[i], consts, scratch, final,
                            head_ahead=1)
        else:
            consts = (whole(norm_mix_g[i]), (pw, j), whole(pool_ls[j])) + ffn_consts
            scratch = ffn_scratch + (pltpu.VMEM((SEQ_TILE, d), jnp.float32),
                                     _shift_buffer(POOL_HALO + SEQ_TILE, d),
                                     pltpu.VMEM((SEQ_TILE, d), jnp.bfloat16))
            x = _layer_call(_pool_layer_kernel, "pool_layer", x, mod[i], consts, scratch, final,
                            head_ahead=0)
    return x
```

```python
import functools

import jax
import jax.numpy as jnp
from jax import lax
from jax.experimental import pallas as pl
from jax.experimental.pallas import tpu as pltpu

EPS = 1e-6
N_ADA = 6
CONV_WIDTH = 31
POOL_WINDOWS = (2, 4, 8, 16)

LANES = 128
SUBLANES = 8
MXU_COLS = 256
CONV_HALO = 32
POOL_HALO = 16
UNIT_ROWS = 64
SEQ_TILE = 512
LAG = 2
ADA_TILE = 1536
CONV_LAYER_SPLITS = dict(up=4, down=2, head=2)
POOL_LAYER_SPLITS = dict(up=2, down=1)
VMEM_LIMIT = (64 * 1024 - 512) * 1024


def _const_spec(a, layer=None):
    if layer is None:
        zeros = (0,) * a.ndim
        return pl.BlockSpec(a.shape, lambda *_: zeros, pipeline_mode=pl.Buffered(1))
    index = (layer,) + (0,) * (a.ndim - 1)
    return pl.BlockSpec((None,) + a.shape[1:], lambda *_: index, pipeline_mode=pl.Buffered(1))


def _rms_mod(x, g, shift, scale):
    ms = jnp.mean(x * x, axis=-1, keepdims=True)
    y = x * lax.rsqrt(ms + EPS) * g
    return y * (1.0 + scale) + shift


def _silu(x):
    return x * jax.nn.sigmoid(x)


ROW_PITCH = 2


def _shift_buffer(rows, d):
    return pltpu.VMEM((d // LANES, ROW_PITCH * rows, LANES), jnp.float32)


def _shift_rows(row0, n):
    return pl.ds(ROW_PITCH * row0, n, stride=ROW_PITCH)


def _shift_store(buf, row0, val):
    for j in range(val.shape[1] // LANES):
        buf[j, _shift_rows(row0, val.shape[0]), :] = val[:, j * LANES:(j + 1) * LANES]


def _shift_zero_halo(buf, halo):
    for j in range(buf.shape[0]):
        buf[j, _shift_rows(0, halo), :] = jnp.zeros((halo, LANES), jnp.float32)


def _shift_carry_halo(buf, halo, tm):
    for j in range(buf.shape[0]):
        buf[j, _shift_rows(0, halo), :] = buf[j, _shift_rows(tm, halo), :]


def _ada_kernel(c_ref, w_ref, b_ref, o_ref):
    ca = _silu(c_ref[...])
    o_ref[0] = jnp.dot(ca, w_ref[0], preferred_element_type=jnp.float32) + b_ref[0]


def _ada_mod(c, ada_w, ada_b):
    depth, d, n = ada_w.shape
    bsz = c.shape[0]
    rows = -(-bsz // SUBLANES) * SUBLANES
    c_pad = jnp.zeros((rows, d), c.dtype).at[:bsz].set(c)
    out = pl.pallas_call(
        _ada_kernel,
        grid=(depth, n // ADA_TILE),
        in_specs=[
            pl.BlockSpec((rows, d), lambda i, j: (0, 0)),
            pl.BlockSpec((1, d, ADA_TILE), lambda i, j: (i, 0, j)),
            pl.BlockSpec((1, 1, ADA_TILE), lambda i, j: (i, 0, j)),
        ],
        out_specs=pl.BlockSpec((1, rows, ADA_TILE), lambda i, j: (i, 0, j)),
        out_shape=jax.ShapeDtypeStruct((depth, rows, n), jnp.float32),
        compiler_params=pltpu.CompilerParams(
            dimension_semantics=("arbitrary", "arbitrary"), vmem_limit_bytes=VMEM_LIMIT),
        name="ada_mod",
    )(c_pad, ada_w, ada_b.reshape(depth, 1, n))
    return out[:, :bsz].reshape(depth, bsz, N_ADA, d)


def _zero_of(anchor):
    bits = pltpu.bitcast(anchor, jnp.uint32)
    return pltpu.bitcast((bits >> 16) >> 16, jnp.float32)


def _anchor(val):
    return val[0:SUBLANES, 0:LANES]


def _operand_after(ref, anchor):
    x = ref[...]
    if anchor is None:
        return x
    zero = _zero_of(anchor)
    top = (x[0:2 * SUBLANES, 0:LANES].astype(jnp.float32)
           + jnp.concatenate([zero, zero], axis=0)).astype(ref.dtype)
    top_rows = jnp.concatenate([top, x[0:2 * SUBLANES, LANES:]], axis=1)
    return jnp.concatenate([top_rows, x[2 * SUBLANES:]], axis=0)


def _deal(items, n):
    return [items[len(items) * k // n:len(items) * (k + 1) // n] for k in range(n)]


def _block_cols(blocks):
    return slice(blocks[0] * MXU_COLS, (blocks[-1] + 1) * MXU_COLS)


def _after(x, anchor):
    reps = (x.shape[0] // SUBLANES, x.shape[1] // LANES)
    return x + jnp.tile(_zero_of(anchor), reps)


def _first_step_init(n, *bufs):
    @pl.when(n == 0)
    def _():
        for buf in bufs:
            buf[...] = jnp.zeros(buf.shape, buf.dtype)


def _mixer_residual(xlag_ref, m, y, gf_ref, o_ref, hb):
    xm = xlag_ref[0] + (1.0 + m[2:3]) * y
    o_ref[0] = xm
    hb[...] = _rms_mod(xm, gf_ref[...], m[3:4], m[4:5]).astype(jnp.bfloat16)


def _ffn_up_blocks(hb, wg_ref, wu_ref, act, blocks, after):
    cols = _block_cols(blocks)
    n = cols.stop - cols.start
    w = jnp.concatenate([wg_ref[:, cols], wu_ref[:, cols]], axis=1)
    ab = jnp.dot(_operand_after(hb, after), w, preferred_element_type=jnp.float32)
    prod = _silu(ab[:, :n]) * ab[:, n:]
    act[:, cols] = prod.astype(jnp.bfloat16)
    return _anchor(prod)


def _ffn_down_blocks(act, m, wd_ref, o_ref, blocks, after):
    cols = _block_cols(blocks)
    y = jnp.dot(_operand_after(act, after), wd_ref[:, cols], preferred_element_type=jnp.float32)
    o_ref[0, :, cols] = o_ref[0, :, cols] + (1.0 + m[5:6, cols]) * y
    return _anchor(y)


def _final_norm(o_ref, fg_ref):
    out = o_ref[0]
    ms = jnp.mean(out * out, axis=-1, keepdims=True)
    o_ref[0] = out * lax.rsqrt(ms + EPS) * fg_ref[...]


def _conv_units(units, wdw_ref, bdw_ref, ubuf, ybuf, after):
    taps = {}
    for j, i in units:
        lanes = slice(j * LANES, (j + 1) * LANES)
        if j not in taps:
            w = wdw_ref[:, lanes]
            taps = {j: [jnp.broadcast_to(w[k:k + 1, :], (UNIT_ROWS, LANES))
                        for k in range(CONV_WIDTH)]}
        acc = jnp.tile(_zero_of(after), (UNIT_ROWS // SUBLANES, 1))
        for k in range(CONV_WIDTH):
            start = CONV_HALO + i * UNIT_ROWS - (CONV_WIDTH - 1) + k
            acc = acc + ubuf[j, _shift_rows(start, UNIT_ROWS), :] * taps[j][k]
        ybuf[i * UNIT_ROWS:(i + 1) * UNIT_ROWS, lanes] = acc + bdw_ref[:, lanes]
        after = _anchor(acc)
    return after


def _conv_layer_kernel(x_ref, xlag_ref, mod_ref, modlag_ref, gm_ref, w1_ref, b1_ref, wdw_ref,
                       bdw_ref, lng_ref, lnb_ref, w2_ref, b2_ref, gf_ref, wg_ref, wu_ref, wd_ref,
                       fg_ref, o_ref, hb, act, ha, gbuf, ubuf, ybuf, *, tiles_per_seq, final):
    n = pl.program_id(0)
    tm, d = x_ref.shape[1], x_ref.shape[2]
    f = act.shape[1]
    up_groups = _deal(list(range(f // MXU_COLS)), CONV_LAYER_SPLITS["up"])
    down_groups = _deal(list(range(d // MXU_COLS)), CONV_LAYER_SPLITS["down"])
    head_groups = _deal(list(range(d // MXU_COLS)), CONV_LAYER_SPLITS["head"])
    units = [(j, i) for j in range(d // LANES) for i in range(tm // UNIT_ROWS)]
    _first_step_init(n, gbuf)

    @pl.when((n == 0) | (lax.rem(n + tiles_per_seq - 1, tiles_per_seq) == 0))
    def _():
        _shift_zero_halo(ubuf, CONV_HALO)

    def head_norm(after):
        m = mod_ref[0]
        x = x_ref[0] if after is None else _after(x_ref[0], after)
        ha[...] = _rms_mod(x, gm_ref[...], m[0:1], m[1:2]).astype(jnp.bfloat16)

    def head_matmul(blocks, after):
        cols = _block_cols(blocks)
        gate_cols = slice(d + cols.start, d + cols.stop)
        nc = cols.stop - cols.start
        w = jnp.concatenate([w1_ref[:, cols], w1_ref[:, gate_cols]], axis=1)
        b = jnp.concatenate([b1_ref[:, cols], b1_ref[:, gate_cols]], axis=1)
        u = jnp.dot(_operand_after(ha, after), w, preferred_element_type=jnp.float32) + b
        gbuf[:, cols] = u[:, :nc] * jax.nn.sigmoid(u[:, nc:])
        return _anchor(u)

    @pl.when(n < LAG)
    def _():
        _shift_store(ubuf, CONV_HALO, gbuf[...])
        head_norm(None)
        for blocks in head_groups:
            head_matmul(blocks, None)
        _conv_units(units, wdw_ref, bdw_ref, ubuf, ybuf, _anchor(x_ref[0]))
        _shift_carry_halo(ubuf, CONV_HALO, tm)

    pl.when(n >= LAG)(functools.partial(
        _conv_layer_step, xlag_ref, modlag_ref, wdw_ref, bdw_ref, lng_ref, lnb_ref, w2_ref,
        b2_ref, gf_ref, wg_ref, wu_ref, wd_ref, fg_ref, o_ref, hb, act, gbuf, ubuf, ybuf,
        head_norm, head_matmul, units, up_groups, down_groups, head_groups, final))


def _conv_layer_step(xlag_ref, modlag_ref, wdw_ref, bdw_ref, lng_ref, lnb_ref, w2_ref, b2_ref,
                     gf_ref, wg_ref, wu_ref, wd_ref, fg_ref, o_ref, hb, act, gbuf, ubuf, ybuf,
                     head_norm, head_matmul, units, up_groups, down_groups, head_groups, final):
    tm = xlag_ref.shape[1]
    n_matmuls = len(up_groups) + len(down_groups) + len(head_groups)
    unit_groups = iter(_deal(units, n_matmuls - 1))
    _shift_store(ubuf, CONV_HALO, gbuf[...])
    v = ybuf[...]
    mu = jnp.mean(v, axis=-1, keepdims=True)
    vc = v - mu
    var = jnp.mean(vc * vc, axis=-1, keepdims=True)
    v = _silu(vc * lax.rsqrt(var + EPS) * lng_ref[...] + lnb_ref[...])
    y = jnp.dot(v.astype(jnp.bfloat16), w2_ref[...],
                preferred_element_type=jnp.float32) + b2_ref[...]
    _mixer_residual(xlag_ref, modlag_ref[0], y, gf_ref, o_ref, hb)

    done, ready = _anchor(y), None
    for k, blocks in enumerate(up_groups):
        done_next = _ffn_up_blocks(hb, wg_ref, wu_ref, act, blocks, ready)
        ready = _conv_units(next(unit_groups), wdw_ref, bdw_ref, ubuf, ybuf, done)
        if k == 0:
            head_norm(ready)
        done = done_next
    for blocks in down_groups:
        done_next = _ffn_down_blocks(act, modlag_ref[0], wd_ref, o_ref, blocks, ready)
        ready = _conv_units(next(unit_groups), wdw_ref, bdw_ref, ubuf, ybuf, done)
        done = done_next
    for k, blocks in enumerate(head_groups):
        done_next = head_matmul(blocks, ready)
        if k + 1 < len(head_groups):
            ready = _conv_units(next(unit_groups), wdw_ref, bdw_ref, ubuf, ybuf, done)
            done = done_next
    _shift_carry_halo(ubuf, CONV_HALO, tm)
    if final:
        _final_norm(o_ref, fg_ref)


def _pool_units(chunks, seq_row0, hbuf, mixbuf, after):
    slabs_per_group = hbuf.shape[0] // len(POOL_WINDOWS)
    for i in chunks:
        row0 = i * UNIT_ROWS
        zero = jnp.tile(_zero_of(after), (UNIT_ROWS // SUBLANES, 1))
        t = seq_row0 + row0 + lax.broadcasted_iota(jnp.int32, (UNIT_ROWS, LANES), 0)
        for gi, win in enumerate(POOL_WINDOWS):
            cnt = jnp.minimum(t + 1, win).astype(jnp.float32)
            for sl in range(gi * slabs_per_group, (gi + 1) * slabs_per_group):
                h = hbuf[sl, _shift_rows(POOL_HALO + row0, UNIT_ROWS), :]
                wsum = h + zero
                for j in range(1, win):
                    wsum = wsum + hbuf[sl, _shift_rows(POOL_HALO + row0 - j, UNIT_ROWS), :]
                mixbuf[row0:row0 + UNIT_ROWS, sl * LANES:(sl + 1) * LANES] = (
                    wsum / cnt - h).astype(jnp.bfloat16)
        after = _anchor(wsum)
    return after


def _pool_layer_kernel(x_ref, xlag_ref, mod_ref, modlag_ref, gm_ref, pw_ref, ls_ref, gf_ref,
                       wg_ref, wu_ref, wd_ref, fg_ref, o_ref, hb, act, hplain, hbuf, mixbuf, *,
                       tiles_per_seq, final):
    n = pl.program_id(0)
    tm, d = x_ref.shape[1], x_ref.shape[2]
    f = act.shape[1]
    window_tile = lax.rem(n + tiles_per_seq - 1, tiles_per_seq)
    up_groups = _deal(list(range(f // MXU_COLS)), POOL_LAYER_SPLITS["up"])
    down_groups = _deal(list(range(d // MXU_COLS)), POOL_LAYER_SPLITS["down"])
    chunks = list(range(tm // UNIT_ROWS))
    _first_step_init(n, hplain)

    @pl.when((n == 0) | (window_tile == 0))
    def _():
        _shift_zero_halo(hbuf, POOL_HALO)

    def head_norm(after):
        m = mod_ref[0]
        h = _rms_mod(_after(x_ref[0], after), gm_ref[...], m[0:1], m[1:2])
        hplain[...] = h
        return _anchor(h)

    @pl.when(n < LAG)
    def _():
        _shift_store(hbuf, POOL_HALO, hplain[...])
        done = _pool_units(chunks, window_tile * tm, hbuf, mixbuf, _anchor(x_ref[0]))
        _shift_carry_halo(hbuf, POOL_HALO, tm)
        head_norm(done)

    pl.when(n >= LAG)(functools.partial(
        _pool_layer_step, xlag_ref, modlag_ref, pw_ref, ls_ref, gf_ref, wg_ref, wu_ref, wd_ref,
        fg_ref, o_ref, hb, act, hplain, hbuf, mixbuf, head_norm, chunks, up_groups, down_groups,
        window_tile * tm, final))


def _pool_layer_step(xlag_ref, modlag_ref, pw_ref, ls_ref, gf_ref, wg_ref, wu_ref, wd_ref, fg_ref,
                     o_ref, hb, act, hplain, hbuf, mixbuf, head_norm, chunks, up_groups,
                     down_groups, seq_row0, final):
    tm, d = xlag_ref.shape[1], xlag_ref.shape[2]
    dg = d // len(POOL_WINDOWS)
    chunk_groups = _deal(chunks, len(up_groups) + len(down_groups) - 1)
    _shift_store(hbuf, POOL_HALO, hplain[...])
    ys = [jnp.dot(mixbuf[:, gi * dg:(gi + 1) * dg], pw_ref[gi],
                  preferred_element_type=jnp.float32) for gi in range(len(POOL_WINDOWS))]
    y = jnp.concatenate(ys, axis=1) * ls_ref[...]
    _mixer_residual(xlag_ref, modlag_ref[0], y, gf_ref, o_ref, hb)

    done, ready = _anchor(y), None
    matmuls = ([functools.partial(_ffn_up_blocks, hb, wg_ref, wu_ref, act, blocks)
                for blocks in up_groups]
               + [functools.partial(_ffn_down_blocks, act, modlag_ref[0], wd_ref, o_ref, blocks)
                  for blocks in down_groups])
    for k, matmul in enumerate(matmuls):
        done_next = matmul(ready)
        if k < len(chunk_groups):
            ready = _pool_units(chunk_groups[k], seq_row0, hbuf, mixbuf, done)
            done = done_next
        if k == len(chunk_groups) - 1:
            ready = head_norm(ready)
    _shift_carry_halo(hbuf, POOL_HALO, tm)
    if final:
        _final_norm(o_ref, fg_ref)


def _layer_call(body, name, x, mod, consts, scratch, final):
    bsz, seq, d = x.shape
    tm = SEQ_TILE
    tiles_per_seq = seq // tm
    n_tiles = bsz * tiles_per_seq
    head_tile = lambda n: jnp.minimum(n, n_tiles - 1)
    tail_tile = lambda n: jnp.maximum(n - LAG, 0)
    xt = x.reshape(n_tiles, tm, d)
    out = pl.pallas_call(
        functools.partial(body, tiles_per_seq=tiles_per_seq, final=final),
        grid=(n_tiles + LAG,),
        in_specs=[
            pl.BlockSpec((1, tm, d), lambda n: (head_tile(n), 0, 0)),
            pl.BlockSpec((1, tm, d), lambda n: (tail_tile(n), 0, 0)),
            pl.BlockSpec((1, N_ADA, d), lambda n: (head_tile(n) // tiles_per_seq, 0, 0)),
            pl.BlockSpec((1, N_ADA, d), lambda n: (tail_tile(n) // tiles_per_seq, 0, 0)),
        ] + [_const_spec(a, layer) for a, layer in consts],
        out_specs=pl.BlockSpec((1, tm, d), lambda n: (tail_tile(n), 0, 0)),
        out_shape=jax.ShapeDtypeStruct((n_tiles, tm, d), x.dtype),
        scratch_shapes=list(scratch),
        compiler_params=pltpu.CompilerParams(
            dimension_semantics=("arbitrary",), vmem_limit_bytes=VMEM_LIMIT),
        name=name,
    )(xt, xt, mod, mod, *[a for a, _ in consts])
    return out.reshape(bsz, seq, d)


def kernel(x, c, ada_w, ada_b, norm_mix_g, norm_ffn_g, conv_w1, conv_b1, conv_wdw, conv_bdw, conv_ln_g, conv_ln_b, conv_w2, conv_b2, pool_w, pool_ls, ffn_w_gate, ffn_w_up, ffn_w_down, final_g):
    depth = ada_w.shape[0]
    d = x.shape[-1]
    f = ffn_w_gate.shape[-1]
    whole = lambda a: (a.reshape(1, -1), None)
    bf16 = lambda a: a.astype(jnp.bfloat16)
    wg, wu, wd = bf16(ffn_w_gate), bf16(ffn_w_up), bf16(ffn_w_down)
    w1, w2, pw = bf16(conv_w1), bf16(conv_w2), bf16(pool_w)
    mod = _ada_mod(c, ada_w, ada_b)
    ffn_scratch = (pltpu.VMEM((SEQ_TILE, d), jnp.bfloat16), pltpu.VMEM((SEQ_TILE, f), jnp.bfloat16))
    for i in range(depth):
        j = i // 2
        final = i == depth - 1
        ffn_consts = (whole(norm_ffn_g[i]), (wg, i), (wu, i), (wd, i), whole(final_g))
        if i % 2 == 0:
            consts = (whole(norm_mix_g[i]), (w1, j), whole(conv_b1[j]), (conv_wdw, j),
                      whole(conv_bdw[j]), whole(conv_ln_g[j]), whole(conv_ln_b[j]),
                      (w2, j), whole(conv_b2[j])) + ffn_consts
            scratch = ffn_scratch + (pltpu.VMEM((SEQ_TILE, d), jnp.bfloat16),
                                     pltpu.VMEM((SEQ_TILE, d), jnp.float32),
                                     _shift_buffer(CONV_HALO + SEQ_TILE, d),
                                     pltpu.VMEM((SEQ_TILE, d), jnp.float32))
            x = _layer_call(_conv_layer_kernel, "conv_layer", x, mod[i], consts, scratch, final)
        else:
            consts = (whole(norm_mix_g[i]), (pw, j), whole(pool_ls[j])) + ffn_consts
            scratch = ffn_scratch + (pltpu.VMEM((SEQ_TILE, d), jnp.float32),
                                     _shift_buffer(POOL_HALO + SEQ_TILE, d),
                                     pltpu.VMEM((SEQ_TILE, d), jnp.bfloat16))
            x = _layer_call(_pool_layer_kernel, "pool_layer", x, mod[i], consts, scratch, final)
    return x
```

```python
import functools

import jax
import jax.numpy as jnp
from jax import lax
from jax.experimental import pallas as pl
from jax.experimental.pallas import tpu as pltpu

EPS = 1e-6
N_ADA = 6
CONV_WIDTH = 31
POOL_WINDOWS = (2, 4, 8, 16)

LANES = 128
SUBLANES = 8
MXU_COLS = 256
CONV_HALO = 32
POOL_HALO = 16
UNIT_ROWS = 64
SEQ_TILE = 512
LAG = 2
ADA_TILE = 1536
CONV_LAYER_SPLITS = dict(up=4, down=2, head=2)
POOL_LAYER_SPLITS = dict(up=2, down=1)
VMEM_LIMIT = 62 * 1024 * 1024


def _const_spec(a, layer=None):
    if layer is None:
        zeros = (0,) * a.ndim
        return pl.BlockSpec(a.shape, lambda *_: zeros, pipeline_mode=pl.Buffered(1))
    index = (layer,) + (0,) * (a.ndim - 1)
    return pl.BlockSpec((None,) + a.shape[1:], lambda *_: index, pipeline_mode=pl.Buffered(1))


def _rms_mod(x, g, shift, scale):
    ms = jnp.mean(x * x, axis=-1, keepdims=True)
    y = x * lax.rsqrt(ms + EPS) * g
    return y * (1.0 + scale) + shift


def _silu(x):
    return x * jax.nn.sigmoid(x)


ROW_PITCH = 2


def _shift_buffer(rows, d):
    return pltpu.VMEM((d // LANES, ROW_PITCH * rows, LANES), jnp.float32)


def _shift_rows(row0, n):
    return pl.ds(ROW_PITCH * row0, n, stride=ROW_PITCH)


def _shift_store(buf, row0, val):
    for j in range(val.shape[1] // LANES):
        buf[j, _shift_rows(row0, val.shape[0]), :] = val[:, j * LANES:(j + 1) * LANES]


def _shift_zero_halo(buf, halo):
    for j in range(buf.shape[0]):
        buf[j, _shift_rows(0, halo), :] = jnp.zeros((halo, LANES), jnp.float32)


def _shift_carry_halo(buf, halo, tm):
    for j in range(buf.shape[0]):
        buf[j, _shift_rows(0, halo), :] = buf[j, _shift_rows(tm, halo), :]


def _ada_kernel(c_ref, w_ref, b_ref, o_ref):
    ca = _silu(c_ref[...])
    o_ref[0] = jnp.dot(ca, w_ref[0], preferred_element_type=jnp.float32) + b_ref[0]


def _ada_mod(c, ada_w, ada_b):
    depth, d, n = ada_w.shape
    bsz = c.shape[0]
    rows = -(-bsz // SUBLANES) * SUBLANES
    c_pad = jnp.zeros((rows, d), c.dtype).at[:bsz].set(c)
    out = pl.pallas_call(
        _ada_kernel,
        grid=(depth, n // ADA_TILE),
        in_specs=[
            pl.BlockSpec((rows, d), lambda i, j: (0, 0)),
            pl.BlockSpec((1, d, ADA_TILE), lambda i, j: (i, 0, j)),
            pl.BlockSpec((1, 1, ADA_TILE), lambda i, j: (i, 0, j)),
        ],
        out_specs=pl.BlockSpec((1, rows, ADA_TILE), lambda i, j: (i, 0, j)),
        out_shape=jax.ShapeDtypeStruct((depth, rows, n), jnp.float32),
        compiler_params=pltpu.CompilerParams(
            dimension_semantics=("arbitrary", "arbitrary"), vmem_limit_bytes=VMEM_LIMIT),
        name="ada_mod",
    )(c_pad, ada_w, ada_b.reshape(depth, 1, n))
    return out[:, :bsz].reshape(depth, bsz, N_ADA, d)


def _zero_of(anchor):
    bits = pltpu.bitcast(anchor, jnp.uint32)
    return pltpu.bitcast((bits >> 16) >> 16, jnp.float32)


def _anchor(val):
    return val[0:SUBLANES, 0:LANES]


def _operand_after(ref, anchor):
    x = ref[...]
    if anchor is None:
        return x
    zero = _zero_of(anchor)
    top = (x[0:2 * SUBLANES, 0:LANES].astype(jnp.float32)
           + jnp.concatenate([zero, zero], axis=0)).astype(ref.dtype)
    top_rows = jnp.concatenate([top, x[0:2 * SUBLANES, LANES:]], axis=1)
    return jnp.concatenate([top_rows, x[2 * SUBLANES:]], axis=0)


def _deal(items, n):
    return [items[len(items) * k // n:len(items) * (k + 1) // n] for k in range(n)]


def _block_cols(blocks):
    return slice(blocks[0] * MXU_COLS, (blocks[-1] + 1) * MXU_COLS)


def _after(x, anchor):
    reps = (x.shape[0] // SUBLANES, x.shape[1] // LANES)
    return x + jnp.tile(_zero_of(anchor), reps)


def _first_step_init(n, *bufs):
    @pl.when(n == 0)
    def _():
        for buf in bufs:
            buf[...] = jnp.zeros(buf.shape, buf.dtype)


def _mixer_residual(xlag_ref, m, y, gf_ref, o_ref, hb):
    xm = xlag_ref[0] + (1.0 + m[2:3]) * y
    o_ref[0] = xm
    hb[...] = _rms_mod(xm, gf_ref[...], m[3:4], m[4:5]).astype(jnp.bfloat16)


def _ffn_up_blocks(hb, wg_ref, wu_ref, act, blocks, after):
    cols = _block_cols(blocks)
    n = cols.stop - cols.start
    w = jnp.concatenate([wg_ref[:, cols], wu_ref[:, cols]], axis=1)
    ab = jnp.dot(_operand_after(hb, after), w, preferred_element_type=jnp.float32)
    prod = _silu(ab[:, :n]) * ab[:, n:]
    act[:, cols] = prod.astype(jnp.bfloat16)
    return _anchor(prod)


def _ffn_down_blocks(act, m, wd_ref, o_ref, blocks, after):
    cols = _block_cols(blocks)
    y = jnp.dot(_operand_after(act, after), wd_ref[:, cols], preferred_element_type=jnp.float32)
    o_ref[0, :, cols] = o_ref[0, :, cols] + (1.0 + m[5:6, cols]) * y
    return _anchor(y)


def _final_norm(o_ref, fg_ref):
    out = o_ref[0]
    ms = jnp.mean(out * out, axis=-1, keepdims=True)
    o_ref[0] = out * lax.rsqrt(ms + EPS) * fg_ref[...]


def _conv_units(units, wdw_ref, bdw_ref, ubuf, ybuf, after):
    taps = {}
    for j, i in units:
        lanes = slice(j * LANES, (j + 1) * LANES)
        if j not in taps:
            w = wdw_ref[:, lanes]
            taps = {j: [jnp.broadcast_to(w[k:k + 1, :], (UNIT_ROWS, LANES))
                        for k in range(CONV_WIDTH)]}
        acc = jnp.tile(_zero_of(after), (UNIT_ROWS // SUBLANES, 1))
        for k in range(CONV_WIDTH):
            start = CONV_HALO + i * UNIT_ROWS - (CONV_WIDTH - 1) + k
            acc = acc + ubuf[j, _shift_rows(start, UNIT_ROWS), :] * taps[j][k]
        ybuf[i * UNIT_ROWS:(i + 1) * UNIT_ROWS, lanes] = acc + bdw_ref[:, lanes]
        after = _anchor(acc)
    return after


def _conv_layer_kernel(x_ref, xlag_ref, mod_ref, modlag_ref, gm_ref, w1_ref, b1_ref, wdw_ref,
                       bdw_ref, lng_ref, lnb_ref, w2_ref, b2_ref, gf_ref, wg_ref, wu_ref, wd_ref,
                       fg_ref, o_ref, hb, act, ha, gbuf, ubuf, ybuf, vbuf, *, tiles_per_seq,
                       final):
    n = pl.program_id(0)
    tm, d = x_ref.shape[1], x_ref.shape[2]
    f = act.shape[1]
    up_groups = _deal(list(range(f // MXU_COLS)), CONV_LAYER_SPLITS["up"])
    down_groups = _deal(list(range(d // MXU_COLS)), CONV_LAYER_SPLITS["down"])
    head_groups = _deal(list(range(d // MXU_COLS)), CONV_LAYER_SPLITS["head"])
    units = [(j, i) for j in range(d // LANES) for i in range(tm // UNIT_ROWS)]
    _first_step_init(n, gbuf, vbuf)

    @pl.when((n == 0) | (lax.rem(n + tiles_per_seq - 1, tiles_per_seq) == 0))
    def _():
        _shift_zero_halo(ubuf, CONV_HALO)

    def head_norm(after):
        m = mod_ref[0]
        x = x_ref[0] if after is None else _after(x_ref[0], after)
        ha[...] = _rms_mod(x, gm_ref[...], m[0:1], m[1:2]).astype(jnp.bfloat16)

    def head_matmul(blocks, after):
        cols = _block_cols(blocks)
        gate_cols = slice(d + cols.start, d + cols.stop)
        nc = cols.stop - cols.start
        w = jnp.concatenate([w1_ref[:, cols], w1_ref[:, gate_cols]], axis=1)
        b = jnp.concatenate([b1_ref[:, cols], b1_ref[:, gate_cols]], axis=1)
        u = jnp.dot(_operand_after(ha, after), w, preferred_element_type=jnp.float32) + b
        gbuf[:, cols] = u[:, :nc] * jax.nn.sigmoid(u[:, nc:])
        return _anchor(u)

    _conv_layer_step(xlag_ref, modlag_ref, wdw_ref, bdw_ref, lng_ref, lnb_ref, w2_ref,
                     b2_ref, gf_ref, wg_ref, wu_ref, wd_ref, fg_ref, o_ref, hb, act, gbuf, ubuf,
                     ybuf, vbuf, head_norm, head_matmul, units, up_groups, down_groups,
                     head_groups, final)


def _conv_layer_step(xlag_ref, modlag_ref, wdw_ref, bdw_ref, lng_ref, lnb_ref, w2_ref, b2_ref,
                     gf_ref, wg_ref, wu_ref, wd_ref, fg_ref, o_ref, hb, act, gbuf, ubuf, ybuf,
                     vbuf, head_norm, head_matmul, units, up_groups, down_groups, head_groups,
                     final):
    tm = xlag_ref.shape[1]
    n_matmuls = len(up_groups) + len(down_groups) + len(head_groups)
    unit_groups = iter(_deal(units, n_matmuls - 1))
    _shift_store(ubuf, CONV_HALO, gbuf[...])
    y = jnp.dot(vbuf[...], w2_ref[...], preferred_element_type=jnp.float32) + b2_ref[...]
    _mixer_residual(xlag_ref, modlag_ref[0], y, gf_ref, o_ref, hb)

    done, ready = _anchor(y), None
    for k, blocks in enumerate(up_groups):
        done_next = _ffn_up_blocks(hb, wg_ref, wu_ref, act, blocks, ready)
        ready = _conv_units(next(unit_groups), wdw_ref, bdw_ref, ubuf, ybuf, done)
        if k == 0:
            head_norm(ready)
        done = done_next
    for blocks in down_groups:
        done_next = _ffn_down_blocks(act, modlag_ref[0], wd_ref, o_ref, blocks, ready)
        ready = _conv_units(next(unit_groups), wdw_ref, bdw_ref, ubuf, ybuf, done)
        done = done_next
    for k, blocks in enumerate(head_groups):
        done_next = head_matmul(blocks, ready)
        if k + 1 < len(head_groups):
            ready = _conv_units(next(unit_groups), wdw_ref, bdw_ref, ubuf, ybuf, done)
            done = done_next
    v = ybuf[...]
    mu = jnp.mean(v, axis=-1, keepdims=True)
    vc = v - mu
    var = jnp.mean(vc * vc, axis=-1, keepdims=True)
    vbuf[...] = _silu(vc * lax.rsqrt(var + EPS) * lng_ref[...]
                      + lnb_ref[...]).astype(jnp.bfloat16)
    _shift_carry_halo(ubuf, CONV_HALO, tm)
    if final:
        _final_norm(o_ref, fg_ref)


def _pool_units(chunks, seq_row0, hbuf, mixbuf, after):
    slabs_per_group = hbuf.shape[0] // len(POOL_WINDOWS)
    for i in chunks:
        row0 = i * UNIT_ROWS
        zero = jnp.tile(_zero_of(after), (UNIT_ROWS // SUBLANES, 1))
        t = seq_row0 + row0 + lax.broadcasted_iota(jnp.int32, (UNIT_ROWS, LANES), 0)
        for gi, win in enumerate(POOL_WINDOWS):
            cnt = jnp.minimum(t + 1, win).astype(jnp.float32)
            for sl in range(gi * slabs_per_group, (gi + 1) * slabs_per_group):
                h = hbuf[sl, _shift_rows(POOL_HALO + row0, UNIT_ROWS), :]
                wsum = h + zero
                for j in range(1, win):
                    wsum = wsum + hbuf[sl, _shift_rows(POOL_HALO + row0 - j, UNIT_ROWS), :]
                mixbuf[row0:row0 + UNIT_ROWS, sl * LANES:(sl + 1) * LANES] = (
                    wsum / cnt - h).astype(jnp.bfloat16)
        after = _anchor(wsum)
    return after


def _pool_layer_kernel(x_ref, xlag_ref, mod_ref, modlag_ref, gm_ref, pw_ref, ls_ref, gf_ref,
                       wg_ref, wu_ref, wd_ref, fg_ref, o_ref, hb, act, hplain, hbuf, mixbuf, *,
                       tiles_per_seq, final):
    n = pl.program_id(0)
    tm, d = x_ref.shape[1], x_ref.shape[2]
    f = act.shape[1]
    window_tile = lax.rem(n + tiles_per_seq - 1, tiles_per_seq)
    up_groups = _deal(list(range(f // MXU_COLS)), POOL_LAYER_SPLITS["up"])
    down_groups = _deal(list(range(d // MXU_COLS)), POOL_LAYER_SPLITS["down"])
    chunks = list(range(tm // UNIT_ROWS))
    _first_step_init(n, hplain)

    @pl.when((n == 0) | (window_tile == 0))
    def _():
        _shift_zero_halo(hbuf, POOL_HALO)

    def head_norm(after):
        m = mod_ref[0]
        h = _rms_mod(_after(x_ref[0], after), gm_ref[...], m[0:1], m[1:2])
        hplain[...] = h
        return _anchor(h)

    @pl.when(n < LAG)
    def _():
        _shift_store(hbuf, POOL_HALO, hplain[...])
        done = _pool_units(chunks, window_tile * tm, hbuf, mixbuf, _anchor(x_ref[0]))
        _shift_carry_halo(hbuf, POOL_HALO, tm)
        head_norm(done)

    pl.when(n >= LAG)(functools.partial(
        _pool_layer_step, xlag_ref, modlag_ref, pw_ref, ls_ref, gf_ref, wg_ref, wu_ref, wd_ref,
        fg_ref, o_ref, hb, act, hplain, hbuf, mixbuf, head_norm, chunks, up_groups, down_groups,
        window_tile * tm, final))


def _pool_layer_step(xlag_ref, modlag_ref, pw_ref, ls_ref, gf_ref, wg_ref, wu_ref, wd_ref, fg_ref,
                     o_ref, hb, act, hplain, hbuf, mixbuf, head_norm, chunks, up_groups,
                     down_groups, seq_row0, final):
    tm, d = xlag_ref.shape[1], xlag_ref.shape[2]
    dg = d // len(POOL_WINDOWS)
    chunk_groups = _deal(chunks, len(up_groups) + len(down_groups) - 1)
    _shift_store(hbuf, POOL_HALO, hplain[...])
    ys = [jnp.dot(mixbuf[:, gi * dg:(gi + 1) * dg], pw_ref[gi],
                  preferred_element_type=jnp.float32) for gi in range(len(POOL_WINDOWS))]
    y = jnp.concatenate(ys, axis=1) * ls_ref[...]
    _mixer_residual(xlag_ref, modlag_ref[0], y, gf_ref, o_ref, hb)

    done, ready = _anchor(y), None
    matmuls = ([functools.partial(_ffn_up_blocks, hb, wg_ref, wu_ref, act, blocks)
                for blocks in up_groups]
               + [functools.partial(_ffn_down_blocks, act, modlag_ref[0], wd_ref, o_ref, blocks)
                  for blocks in down_groups])
    for k, matmul in enumerate(matmuls):
        done_next = matmul(ready)
        if k < len(chunk_groups):
            ready = _pool_units(chunk_groups[k], seq_row0, hbuf, mixbuf, done)
            done = done_next
        if k == len(chunk_groups) - 1:
            ready = head_norm(ready)
    _shift_carry_halo(hbuf, POOL_HALO, tm)
    if final:
        _final_norm(o_ref, fg_ref)


def _layer_call(body, name, x, mod, consts, scratch, final):
    bsz, seq, d = x.shape
    tm = SEQ_TILE
    tiles_per_seq = seq // tm
    n_tiles = bsz * tiles_per_seq
    head_tile = lambda n: jnp.minimum(n, n_tiles - 1)
    tail_tile = lambda n: jnp.maximum(n - LAG, 0)
    xt = x.reshape(n_tiles, tm, d)
    out = pl.pallas_call(
        functools.partial(body, tiles_per_seq=tiles_per_seq, final=final),
        grid=(n_tiles + LAG,),
        in_specs=[
            pl.BlockSpec((1, tm, d), lambda n: (head_tile(n), 0, 0)),
            pl.BlockSpec((1, tm, d), lambda n: (tail_tile(n), 0, 0)),
            pl.BlockSpec((1, N_ADA, d), lambda n: (head_tile(n) // tiles_per_seq, 0, 0)),
            pl.BlockSpec((1, N_ADA, d), lambda n: (tail_tile(n) // tiles_per_seq, 0, 0)),
        ] + [_const_spec(a, layer) for a, layer in consts],
        out_specs=pl.BlockSpec((1, tm, d), lambda n: (tail_tile(n), 0, 0)),
        out_shape=jax.ShapeDtypeStruct((n_tiles, tm, d), x.dtype),
        scratch_shapes=list(scratch),
        compiler_params=pltpu.CompilerParams(
            dimension_semantics=("arbitrary",), vmem_limit_bytes=VMEM_LIMIT),
        name=name,
    )(xt, xt, mod, mod, *[a for a, _ in consts])
    return out.reshape(bsz, seq, d)


def kernel(x, c, ada_w, ada_b, norm_mix_g, norm_ffn_g, conv_w1, conv_b1, conv_wdw, conv_bdw, conv_ln_g, conv_ln_b, conv_w2, conv_b2, pool_w, pool_ls, ffn_w_gate, ffn_w_up, ffn_w_down, final_g):
    depth = ada_w.shape[0]
    d = x.shape[-1]
    f = ffn_w_gate.shape[-1]
    whole = lambda a: (a.reshape(1, -1), None)
    bf16 = lambda a: a.astype(jnp.bfloat16)
    wg, wu, wd = bf16(ffn_w_gate), bf16(ffn_w_up), bf16(ffn_w_down)
    w1, w2, pw = bf16(conv_w1), bf16(conv_w2), bf16(pool_w)
    mod = _ada_mod(c, ada_w, ada_b)
    ffn_scratch = (pltpu.VMEM((SEQ_TILE, d), jnp.bfloat16), pltpu.VMEM((SEQ_TILE, f), jnp.bfloat16))
    for i in range(depth):
        j = i // 2
        final = i == depth - 1
        ffn_consts = (whole(norm_ffn_g[i]), (wg, i), (wu, i), (wd, i), whole(final_g))
        if i % 2 == 0:
            consts = (whole(norm_mix_g[i]), (w1, j), whole(conv_b1[j]), (conv_wdw, j),
                      whole(conv_bdw[j]), whole(conv_ln_g[j]), whole(conv_ln_b[j]),
                      (w2, j), whole(conv_b2[j])) + ffn_consts
            scratch = ffn_scratch + (pltpu.VMEM((SEQ_TILE, d), jnp.bfloat16),
                                     pltpu.VMEM((SEQ_TILE, d), jnp.float32),
                                     _shift_buffer(CONV_HALO + SEQ_TILE, d),
                                     pltpu.VMEM((SEQ_TILE, d), jnp.float32),
                                     pltpu.VMEM((SEQ_TILE, d), jnp.bfloat16))
            x = _layer_call(_conv_layer_kernel, "conv_layer", x, mod[i], consts, scratch, final)
        else:
            consts = (whole(norm_mix_g[i]), (pw, j), whole(pool_ls[j])) + ffn_consts
            scratch = ffn_scratch + (pltpu.VMEM((SEQ_TILE, d), jnp.float32),
                                     _shift_buffer(POOL_HALO + SEQ_TILE, d),
                                     pltpu.VMEM((SEQ_TILE, d), jnp.bfloat16))
            x = _layer_call(_pool_layer_kernel, "pool_layer", x, mod[i], consts, scratch, final)
    return x
```

```python
import functools

import jax
import jax.numpy as jnp
from jax import lax
from jax.experimental import pallas as pl
from jax.experimental.pallas import tpu as pltpu

EPS = 1e-6
N_ADA = 6
CONV_WIDTH = 31
POOL_WINDOWS = (2, 4, 8, 16)

LANES = 128
SUBLANES = 8
MXU_COLS = 256
CONV_HALO = 32
POOL_HALO = 16
UNIT_ROWS = 64
SEQ_TILE = 512
LAG = 2
ADA_TILE = 1536
CONV_LAYER_SPLITS = dict(up=4, down=2, head=2)
POOL_LAYER_SPLITS = dict(up=2, down=1)
VMEM_LIMIT = 62 * 1024 * 1024


def _const_spec(a, layer=None):
    if layer is None:
        zeros = (0,) * a.ndim
        return pl.BlockSpec(a.shape, lambda *_: zeros, pipeline_mode=pl.Buffered(1))
    index = (layer,) + (0,) * (a.ndim - 1)
    return pl.BlockSpec((None,) + a.shape[1:], lambda *_: index, pipeline_mode=pl.Buffered(1))


def _rms_mod(x, g, shift, scale):
    ms = jnp.mean(x * x, axis=-1, keepdims=True)
    y = x * lax.rsqrt(ms + EPS) * g
    return y * (1.0 + scale) + shift


def _silu(x):
    return x * jax.nn.sigmoid(x)


ROW_PITCH = 2


def _shift_buffer(rows, d):
    return pltpu.VMEM((d // LANES, ROW_PITCH * rows, LANES), jnp.float32)


def _shift_rows(row0, n):
    return pl.ds(ROW_PITCH * row0, n, stride=ROW_PITCH)


def _shift_store(buf, row0, val):
    for j in range(val.shape[1] // LANES):
        buf[j, _shift_rows(row0, val.shape[0]), :] = val[:, j * LANES:(j + 1) * LANES]


def _shift_zero_halo(buf, halo):
    for j in range(buf.shape[0]):
        buf[j, _shift_rows(0, halo), :] = jnp.zeros((halo, LANES), jnp.float32)


def _shift_carry_halo(buf, halo, tm):
    for j in range(buf.shape[0]):
        buf[j, _shift_rows(0, halo), :] = buf[j, _shift_rows(tm, halo), :]


def _ada_kernel(c_ref, w_ref, b_ref, o_ref):
    ca = _silu(c_ref[...])
    o_ref[0] = jnp.dot(ca, w_ref[0], preferred_element_type=jnp.float32) + b_ref[0]


def _ada_mod(c, ada_w, ada_b):
    depth, d, n = ada_w.shape
    bsz = c.shape[0]
    rows = -(-bsz // SUBLANES) * SUBLANES
    c_pad = jnp.zeros((rows, d), c.dtype).at[:bsz].set(c)
    out = pl.pallas_call(
        _ada_kernel,
        grid=(depth, n // ADA_TILE),
        in_specs=[
            pl.BlockSpec((rows, d), lambda i, j: (0, 0)),
            pl.BlockSpec((1, d, ADA_TILE), lambda i, j: (i, 0, j)),
            pl.BlockSpec((1, 1, ADA_TILE), lambda i, j: (i, 0, j)),
        ],
        out_specs=pl.BlockSpec((1, rows, ADA_TILE), lambda i, j: (i, 0, j)),
        out_shape=jax.ShapeDtypeStruct((depth, rows, n), jnp.float32),
        compiler_params=pltpu.CompilerParams(
            dimension_semantics=("arbitrary", "arbitrary"), vmem_limit_bytes=VMEM_LIMIT),
        name="ada_mod",
    )(c_pad, ada_w, ada_b.reshape(depth, 1, n))
    return out[:, :bsz].reshape(depth, bsz, N_ADA, d)


def _zero_of(anchor):
    bits = pltpu.bitcast(anchor, jnp.uint32)
    return pltpu.bitcast((bits >> 16) >> 16, jnp.float32)


def _anchor(val):
    return val[0:SUBLANES, 0:LANES]


def _operand_after(ref, anchor):
    x = ref[...]
    if anchor is None:
        return x
    zero = _zero_of(anchor)
    top = (x[0:2 * SUBLANES, 0:LANES].astype(jnp.float32)
           + jnp.concatenate([zero, zero], axis=0)).astype(ref.dtype)
    top_rows = jnp.concatenate([top, x[0:2 * SUBLANES, LANES:]], axis=1)
    return jnp.concatenate([top_rows, x[2 * SUBLANES:]], axis=0)


def _deal(items, n):
    return [items[len(items) * k // n:len(items) * (k + 1) // n] for k in range(n)]


def _block_cols(blocks):
    return slice(blocks[0] * MXU_COLS, (blocks[-1] + 1) * MXU_COLS)


def _after(x, anchor):
    reps = (x.shape[0] // SUBLANES, x.shape[1] // LANES)
    return x + jnp.tile(_zero_of(anchor), reps)


def _first_step_init(n, *bufs):
    @pl.when(n == 0)
    def _():
        for buf in bufs:
            buf[...] = jnp.zeros(buf.shape, buf.dtype)


def _mixer_residual(xlag_ref, m, y, gf_ref, o_ref, hb):
    xm = xlag_ref[0] + (1.0 + m[2:3]) * y
    o_ref[0] = xm
    hb[...] = _rms_mod(xm, gf_ref[...], m[3:4], m[4:5]).astype(jnp.bfloat16)


def _ffn_up_blocks(hb, wg_ref, wu_ref, act, blocks, after):
    cols = _block_cols(blocks)
    n = cols.stop - cols.start
    w = jnp.concatenate([wg_ref[:, cols], wu_ref[:, cols]], axis=1)
    ab = jnp.dot(_operand_after(hb, after), w, preferred_element_type=jnp.float32)
    prod = _silu(ab[:, :n]) * ab[:, n:]
    act[:, cols] = prod.astype(jnp.bfloat16)
    return _anchor(prod)


def _ffn_down_blocks(act, m, wd_ref, o_ref, blocks, after):
    cols = _block_cols(blocks)
    y = jnp.dot(_operand_after(act, after), wd_ref[:, cols], preferred_element_type=jnp.float32)
    o_ref[0, :, cols] = o_ref[0, :, cols] + (1.0 + m[5:6, cols]) * y
    return _anchor(y)


def _final_norm(o_ref, fg_ref):
    out = o_ref[0]
    ms = jnp.mean(out * out, axis=-1, keepdims=True)
    o_ref[0] = out * lax.rsqrt(ms + EPS) * fg_ref[...]


def _conv_units(units, wdw_ref, bdw_ref, ubuf, ybuf, after):
    taps = {}
    for j, i in units:
        lanes = slice(j * LANES, (j + 1) * LANES)
        if j not in taps:
            w = wdw_ref[:, lanes]
            taps = {j: [jnp.broadcast_to(w[k:k + 1, :], (UNIT_ROWS, LANES))
                        for k in range(CONV_WIDTH)]}
        acc = jnp.tile(_zero_of(after), (UNIT_ROWS // SUBLANES, 1))
        for k in range(CONV_WIDTH):
            start = CONV_HALO + i * UNIT_ROWS - (CONV_WIDTH - 1) + k
            acc = acc + ubuf[j, _shift_rows(start, UNIT_ROWS), :] * taps[j][k]
        ybuf[i * UNIT_ROWS:(i + 1) * UNIT_ROWS, lanes] = acc + bdw_ref[:, lanes]
        after = _anchor(acc)
    return after


def _conv_layer_kernel(x_ref, xlag_ref, mod_ref, modlag_ref, gm_ref, w1_ref, b1_ref, wdw_ref,
                       bdw_ref, lng_ref, lnb_ref, w2_ref, b2_ref, gf_ref, wg_ref, wu_ref, wd_ref,
                       fg_ref, o_ref, hb, act, ha, gbuf, ubuf, ybuf, vbuf, *, tiles_per_seq,
                       final):
    n = pl.program_id(0)
    tm, d = x_ref.shape[1], x_ref.shape[2]
    f = act.shape[1]
    up_groups = _deal(list(range(f // MXU_COLS)), CONV_LAYER_SPLITS["up"])
    down_groups = _deal(list(range(d // MXU_COLS)), CONV_LAYER_SPLITS["down"])
    head_groups = _deal(list(range(d // MXU_COLS)), CONV_LAYER_SPLITS["head"])
    units = [(j, i) for j in range(d // LANES) for i in range(tm // UNIT_ROWS)]
    _first_step_init(n, gbuf, vbuf)

    @pl.when((n == 0) | (lax.rem(n + tiles_per_seq - 1, tiles_per_seq) == 0))
    def _():
        _shift_zero_halo(ubuf, CONV_HALO)

    def head_norm(after):
        m = mod_ref[0]
        x = x_ref[0] if after is None else _after(x_ref[0], after)
        ha[...] = _rms_mod(x, gm_ref[...], m[0:1], m[1:2]).astype(jnp.bfloat16)

    def head_matmul(blocks, after):
        cols = _block_cols(blocks)
        gate_cols = slice(d + cols.start, d + cols.stop)
        nc = cols.stop - cols.start
        w = jnp.concatenate([w1_ref[:, cols], w1_ref[:, gate_cols]], axis=1)
        b = jnp.concatenate([b1_ref[:, cols], b1_ref[:, gate_cols]], axis=1)
        u = jnp.dot(_operand_after(ha, after), w, preferred_element_type=jnp.float32) + b
        gbuf[:, cols] = u[:, :nc] * jax.nn.sigmoid(u[:, nc:])
        return _anchor(u)

    _conv_layer_step(xlag_ref, modlag_ref, wdw_ref, bdw_ref, lng_ref, lnb_ref, w2_ref,
                     b2_ref, gf_ref, wg_ref, wu_ref, wd_ref, fg_ref, o_ref, hb, act, gbuf, ubuf,
                     ybuf, vbuf, head_norm, head_matmul, units, up_groups, down_groups,
                     head_groups, final)


def _conv_layer_step(xlag_ref, modlag_ref, wdw_ref, bdw_ref, lng_ref, lnb_ref, w2_ref, b2_ref,
                     gf_ref, wg_ref, wu_ref, wd_ref, fg_ref, o_ref, hb, act, gbuf, ubuf, ybuf,
                     vbuf, head_norm, head_matmul, units, up_groups, down_groups, head_groups,
                     final):
    tm = xlag_ref.shape[1]
    unit_groups = iter(_deal(units, len(up_groups) + len(down_groups)))
    _shift_store(ubuf, CONV_HALO, gbuf[...])
    y = jnp.dot(vbuf[...], w2_ref[...], preferred_element_type=jnp.float32) + b2_ref[...]
    _mixer_residual(xlag_ref, modlag_ref[0], y, gf_ref, o_ref, hb)

    done, ready = _anchor(y), None
    for k, blocks in enumerate(up_groups):
        done_next = _ffn_up_blocks(hb, wg_ref, wu_ref, act, blocks, ready)
        ready = _conv_units(next(unit_groups), wdw_ref, bdw_ref, ubuf, ybuf, done)
        if k == 0:
            head_norm(ready)
        done = done_next
    for blocks in down_groups:
        done_next = _ffn_down_blocks(act, modlag_ref[0], wd_ref, o_ref, blocks, ready)
        ready = _conv_units(next(unit_groups), wdw_ref, bdw_ref, ubuf, ybuf, done)
        done = done_next
    for k, blocks in enumerate(head_groups):
        head_matmul(blocks, ready)
        if k == 0:
            v = ybuf[...]
            mu = jnp.mean(v, axis=-1, keepdims=True)
            vc = v - mu
            var = jnp.mean(vc * vc, axis=-1, keepdims=True)
            v = _silu(vc * lax.rsqrt(var + EPS) * lng_ref[...] + lnb_ref[...])
            vbuf[...] = v.astype(jnp.bfloat16)
            s = sum(v[:, j * LANES:(j + 1) * LANES] for j in range(v.shape[1] // LANES))
            ready = sum(s[i * SUBLANES:(i + 1) * SUBLANES] for i in range(tm // SUBLANES))
    _shift_carry_halo(ubuf, CONV_HALO, tm)
    if final:
        _final_norm(o_ref, fg_ref)


def _pool_units(chunks, seq_row0, hbuf, mixbuf, after):
    slabs_per_group = hbuf.shape[0] // len(POOL_WINDOWS)
    for i in chunks:
        row0 = i * UNIT_ROWS
        zero = jnp.tile(_zero_of(after), (UNIT_ROWS // SUBLANES, 1))
        t = seq_row0 + row0 + lax.broadcasted_iota(jnp.int32, (UNIT_ROWS, LANES), 0)
        for gi, win in enumerate(POOL_WINDOWS):
            cnt = jnp.minimum(t + 1, win).astype(jnp.float32)
            for sl in range(gi * slabs_per_group, (gi + 1) * slabs_per_group):
                h = hbuf[sl, _shift_rows(POOL_HALO + row0, UNIT_ROWS), :]
                wsum = h + zero
                for j in range(1, win):
                    wsum = wsum + hbuf[sl, _shift_rows(POOL_HALO + row0 - j, UNIT_ROWS), :]
                mixbuf[row0:row0 + UNIT_ROWS, sl * LANES:(sl + 1) * LANES] = (
                    wsum / cnt - h).astype(jnp.bfloat16)
        after = _anchor(wsum)
    return after


def _pool_layer_kernel(x_ref, xlag_ref, mod_ref, modlag_ref, gm_ref, pw_ref, ls_ref, gf_ref,
                       wg_ref, wu_ref, wd_ref, fg_ref, o_ref, hb, act, hplain, hbuf, mixbuf, *,
                       tiles_per_seq, final):
    n = pl.program_id(0)
    tm, d = x_ref.shape[1], x_ref.shape[2]
    f = act.shape[1]
    window_tile = lax.rem(n + tiles_per_seq - 1, tiles_per_seq)
    up_groups = _deal(list(range(f // MXU_COLS)), POOL_LAYER_SPLITS["up"])
    down_groups = _deal(list(range(d // MXU_COLS)), POOL_LAYER_SPLITS["down"])
    chunks = list(range(tm // UNIT_ROWS))
    _first_step_init(n, hplain)

    @pl.when((n == 0) | (window_tile == 0))
    def _():
        _shift_zero_halo(hbuf, POOL_HALO)

    def head_norm(after):
        m = mod_ref[0]
        h = _rms_mod(_after(x_ref[0], after), gm_ref[...], m[0:1], m[1:2])
        hplain[...] = h
        return _anchor(h)

    @pl.when(n < LAG)
    def _():
        _shift_store(hbuf, POOL_HALO, hplain[...])
        done = _pool_units(chunks, window_tile * tm, hbuf, mixbuf, _anchor(x_ref[0]))
        _shift_carry_halo(hbuf, POOL_HALO, tm)
        head_norm(done)

    pl.when(n >= LAG)(functools.partial(
        _pool_layer_step, xlag_ref, modlag_ref, pw_ref, ls_ref, gf_ref, wg_ref, wu_ref, wd_ref,
        fg_ref, o_ref, hb, act, hplain, hbuf, mixbuf, head_norm, chunks, up_groups, down_groups,
        window_tile * tm, final))


def _pool_layer_step(xlag_ref, modlag_ref, pw_ref, ls_ref, gf_ref, wg_ref, wu_ref, wd_ref, fg_ref,
                     o_ref, hb, act, hplain, hbuf, mixbuf, head_norm, chunks, up_groups,
                     down_groups, seq_row0, final):
    tm, d = xlag_ref.shape[1], xlag_ref.shape[2]
    dg = d // len(POOL_WINDOWS)
    chunk_groups = _deal(chunks, len(up_groups) + len(down_groups) - 1)
    _shift_store(hbuf, POOL_HALO, hplain[...])
    ys = [jnp.dot(mixbuf[:, gi * dg:(gi + 1) * dg], pw_ref[gi],
                  preferred_element_type=jnp.float32) for gi in range(len(POOL_WINDOWS))]
    y = jnp.concatenate(ys, axis=1) * ls_ref[...]
    _mixer_residual(xlag_ref, modlag_ref[0], y, gf_ref, o_ref, hb)

    done, ready = _anchor(y), None
    matmuls = ([functools.partial(_ffn_up_blocks, hb, wg_ref, wu_ref, act, blocks)
                for blocks in up_groups]
               + [functools.partial(_ffn_down_blocks, act, modlag_ref[0], wd_ref, o_ref, blocks)
                  for blocks in down_groups])
    for k, matmul in enumerate(matmuls):
        done_next = matmul(ready)
        if k < len(chunk_groups):
            ready = _pool_units(chunk_groups[k], seq_row0, hbuf, mixbuf, done)
            done = done_next
        if k == len(chunk_groups) - 1:
            ready = head_norm(ready)
    _shift_carry_halo(hbuf, POOL_HALO, tm)
    if final:
        _final_norm(o_ref, fg_ref)


def _layer_call(body, name, x, mod, consts, scratch, final):
    bsz, seq, d = x.shape
    tm = SEQ_TILE
    tiles_per_seq = seq // tm
    n_tiles = bsz * tiles_per_seq
    head_tile = lambda n: jnp.minimum(n, n_tiles - 1)
    tail_tile = lambda n: jnp.maximum(n - LAG, 0)
    xt = x.reshape(n_tiles, tm, d)
    out = pl.pallas_call(
        functools.partial(body, tiles_per_seq=tiles_per_seq, final=final),
        grid=(n_tiles + LAG,),
        in_specs=[
            pl.BlockSpec((1, tm, d), lambda n: (head_tile(n), 0, 0)),
            pl.BlockSpec((1, tm, d), lambda n: (tail_tile(n), 0, 0)),
            pl.BlockSpec((1, N_ADA, d), lambda n: (head_tile(n) // tiles_per_seq, 0, 0)),
            pl.BlockSpec((1, N_ADA, d), lambda n: (tail_tile(n) // tiles_per_seq, 0, 0)),
        ] + [_const_spec(a, layer) for a, layer in consts],
        out_specs=pl.BlockSpec((1, tm, d), lambda n: (tail_tile(n), 0, 0)),
        out_shape=jax.ShapeDtypeStruct((n_tiles, tm, d), x.dtype),
        scratch_shapes=list(scratch),
        compiler_params=pltpu.CompilerParams(
            dimension_semantics=("arbitrary",), vmem_limit_bytes=VMEM_LIMIT),
        name=name,
    )(xt, xt, mod, mod, *[a for a, _ in consts])
    return out.reshape(bsz, seq, d)


def kernel(x, c, ada_w, ada_b, norm_mix_g, norm_ffn_g, conv_w1, conv_b1, conv_wdw, conv_bdw, conv_ln_g, conv_ln_b, conv_w2, conv_b2, pool_w, pool_ls, ffn_w_gate, ffn_w_up, ffn_w_down, final_g):
    depth = ada_w.shape[0]
    d = x.shape[-1]
    f = ffn_w_gate.shape[-1]
    whole = lambda a: (a.reshape(1, -1), None)
    bf16 = lambda a: a.astype(jnp.bfloat16)
    wg, wu, wd = bf16(ffn_w_gate), bf16(ffn_w_up), bf16(ffn_w_down)
    w1, w2, pw = bf16(conv_w1), bf16(conv_w2), bf16(pool_w)
    mod = _ada_mod(c, ada_w, ada_b)
    ffn_scratch = (pltpu.VMEM((SEQ_TILE, d), jnp.bfloat16), pltpu.VMEM((SEQ_TILE, f), jnp.bfloat16))
    for i in range(depth):
        j = i // 2
        final = i == depth - 1
        ffn_consts = (whole(norm_ffn_g[i]), (wg, i), (wu, i), (wd, i), whole(final_g))
        if i % 2 == 0:
            consts = (whole(norm_mix_g[i]), (w1, j), whole(conv_b1[j]), (conv_wdw, j),
                      whole(conv_bdw[j]), whole(conv_ln_g[j]), whole(conv_ln_b[j]),
                      (w2, j), whole(conv_b2[j])) + ffn_consts
            scratch = ffn_scratch + (pltpu.VMEM((SEQ_TILE, d), jnp.bfloat16),
                                     pltpu.VMEM((SEQ_TILE, d), jnp.float32),
                                     _shift_buffer(CONV_HALO + SEQ_TILE, d),
                                     pltpu.VMEM((SEQ_TILE, d), jnp.float32),
                                     pltpu.VMEM((SEQ_TILE, d), jnp.bfloat16))
            x = _layer_call(_conv_layer_kernel, "conv_layer", x, mod[i], consts, scratch, final)
        else:
            consts = (whole(norm_mix_g[i]), (pw, j), whole(pool_ls[j])) + ffn_consts
            scratch = ffn_scratch + (pltpu.VMEM((SEQ_TILE, d), jnp.float32),
                                     _shift_buffer(POOL_HALO + SEQ_TILE, d),
                                     pltpu.VMEM((SEQ_TILE, d), jnp.bfloat16))
            x = _layer_call(_pool_layer_kernel, "pool_layer", x, mod[i], consts, scratch, final)
    return x
```

```python
import functools

import jax
import jax.numpy as jnp
from jax import lax
from jax.experimental import pallas as pl
from jax.experimental.pallas import tpu as pltpu

EPS = 1e-6
N_ADA = 6
CONV_WIDTH = 31
POOL_WINDOWS = (2, 4, 8, 16)

LANES = 128
SUBLANES = 8
MXU_COLS = 256
CONV_HALO = 32
POOL_HALO = 16
UNIT_ROWS = 64
SEQ_TILE = 512
LAG = 2
ADA_TILE = 1536
CONV_LAYER_SPLITS = dict(up=4, down=2, head=2)
POOL_LAYER_SPLITS = dict(up=2, down=1)
VMEM_LIMIT = 62 * 1024 * 1024


def _const_spec(a, layer=None):
    if layer is None:
        zeros = (0,) * a.ndim
        return pl.BlockSpec(a.shape, lambda *_: zeros, pipeline_mode=pl.Buffered(1))
    index = (layer,) + (0,) * (a.ndim - 1)
    return pl.BlockSpec((None,) + a.shape[1:], lambda *_: index, pipeline_mode=pl.Buffered(1))


def _rms_mod(x, g, shift, scale):
    ms = jnp.mean(x * x, axis=-1, keepdims=True)
    y = x * lax.rsqrt(ms + EPS) * g
    return y * (1.0 + scale) + shift


def _silu(x):
    return x * jax.nn.sigmoid(x)


ROW_PITCH = 2


def _shift_buffer(rows, d):
    return pltpu.VMEM((d // LANES, ROW_PITCH * rows, LANES), jnp.float32)


def _shift_rows(row0, n):
    return pl.ds(ROW_PITCH * row0, n, stride=ROW_PITCH)


def _shift_store(buf, row0, val):
    for j in range(val.shape[1] // LANES):
        buf[j, _shift_rows(row0, val.shape[0]), :] = val[:, j * LANES:(j + 1) * LANES]


def _shift_zero_halo(buf, halo):
    for j in range(buf.shape[0]):
        buf[j, _shift_rows(0, halo), :] = jnp.zeros((halo, LANES), jnp.float32)


def _shift_carry_halo(buf, halo, tm):
    for j in range(buf.shape[0]):
        buf[j, _shift_rows(0, halo), :] = buf[j, _shift_rows(tm, halo), :]


def _ada_kernel(c_ref, w_ref, b_ref, o_ref):
    ca = _silu(c_ref[...])
    o_ref[0] = jnp.dot(ca, w_ref[0], preferred_element_type=jnp.float32) + b_ref[0]


def _ada_mod(c, ada_w, ada_b):
    depth, d, n = ada_w.shape
    bsz = c.shape[0]
    rows = -(-bsz // SUBLANES) * SUBLANES
    c_pad = jnp.zeros((rows, d), c.dtype).at[:bsz].set(c)
    out = pl.pallas_call(
        _ada_kernel,
        grid=(depth, n // ADA_TILE),
        in_specs=[
            pl.BlockSpec((rows, d), lambda i, j: (0, 0)),
            pl.BlockSpec((1, d, ADA_TILE), lambda i, j: (i, 0, j)),
            pl.BlockSpec((1, 1, ADA_TILE), lambda i, j: (i, 0, j)),
        ],
        out_specs=pl.BlockSpec((1, rows, ADA_TILE), lambda i, j: (i, 0, j)),
        out_shape=jax.ShapeDtypeStruct((depth, rows, n), jnp.float32),
        compiler_params=pltpu.CompilerParams(
            dimension_semantics=("arbitrary", "arbitrary"), vmem_limit_bytes=VMEM_LIMIT),
        name="ada_mod",
    )(c_pad, ada_w, ada_b.reshape(depth, 1, n))
    return out[:, :bsz].reshape(depth, bsz, N_ADA, d)


def _zero_of(anchor):
    bits = pltpu.bitcast(anchor, jnp.uint32)
    return pltpu.bitcast((bits >> 16) >> 16, jnp.float32)


def _anchor(val):
    return val[0:SUBLANES, 0:LANES]


def _operand_after(ref, anchor):
    x = ref[...]
    if anchor is None:
        return x
    zero = _zero_of(anchor)
    top = (x[0:2 * SUBLANES, 0:LANES].astype(jnp.float32)
           + jnp.concatenate([zero, zero], axis=0)).astype(ref.dtype)
    top_rows = jnp.concatenate([top, x[0:2 * SUBLANES, LANES:]], axis=1)
    return jnp.concatenate([top_rows, x[2 * SUBLANES:]], axis=0)


def _deal(items, n):
    return [items[len(items) * k // n:len(items) * (k + 1) // n] for k in range(n)]


def _block_cols(blocks):
    return slice(blocks[0] * MXU_COLS, (blocks[-1] + 1) * MXU_COLS)


def _after(x, anchor):
    reps = (x.shape[0] // SUBLANES, x.shape[1] // LANES)
    return x + jnp.tile(_zero_of(anchor), reps)


def _first_step_init(n, *bufs):
    @pl.when(n == 0)
    def _():
        for buf in bufs:
            buf[...] = jnp.zeros(buf.shape, buf.dtype)


def _mixer_residual(xlag_ref, m, y, gf_ref, o_ref, hb):
    xm = xlag_ref[0] + (1.0 + m[2:3]) * y
    o_ref[0] = xm
    hb[...] = _rms_mod(xm, gf_ref[...], m[3:4], m[4:5]).astype(jnp.bfloat16)


def _ffn_up_blocks(hb, wg_ref, wu_ref, act, blocks, after):
    cols = _block_cols(blocks)
    n = cols.stop - cols.start
    w = jnp.concatenate([wg_ref[:, cols], wu_ref[:, cols]], axis=1)
    ab = jnp.dot(_operand_after(hb, after), w, preferred_element_type=jnp.float32)
    prod = _silu(ab[:, :n]) * ab[:, n:]
    act[:, cols] = prod.astype(jnp.bfloat16)
    return _anchor(prod)


def _ffn_down_blocks(act, m, wd_ref, o_ref, blocks, after):
    cols = _block_cols(blocks)
    y = jnp.dot(_operand_after(act, after), wd_ref[:, cols], preferred_element_type=jnp.float32)
    o_ref[0, :, cols] = o_ref[0, :, cols] + (1.0 + m[5:6, cols]) * y
    return _anchor(y)


def _final_norm(o_ref, fg_ref):
    out = o_ref[0]
    ms = jnp.mean(out * out, axis=-1, keepdims=True)
    o_ref[0] = out * lax.rsqrt(ms + EPS) * fg_ref[...]


def _conv_units(units, wdw_ref, bdw_ref, ubuf, ybuf, after):
    taps = {}
    for j, i in units:
        lanes = slice(j * LANES, (j + 1) * LANES)
        if j not in taps:
            w = wdw_ref[:, lanes]
            taps = {j: [jnp.broadcast_to(w[k:k + 1, :], (UNIT_ROWS, LANES))
                        for k in range(CONV_WIDTH)]}
        acc = jnp.tile(_zero_of(after), (UNIT_ROWS // SUBLANES, 1))
        for k in range(CONV_WIDTH):
            start = CONV_HALO + i * UNIT_ROWS - (CONV_WIDTH - 1) + k
            acc = acc + ubuf[j, _shift_rows(start, UNIT_ROWS), :] * taps[j][k]
        ybuf[i * UNIT_ROWS:(i + 1) * UNIT_ROWS, lanes] = acc + bdw_ref[:, lanes]
        after = _anchor(acc)
    return after


def _conv_layer_kernel(x_ref, xlag_ref, mod_ref, modlag_ref, gm_ref, w1_ref, b1_ref, wdw_ref,
                       bdw_ref, lng_ref, lnb_ref, w2_ref, b2_ref, gf_ref, wg_ref, wu_ref, wd_ref,
                       fg_ref, o_ref, hb, act, ha, gbuf, ubuf, ybuf, vbuf, *, tiles_per_seq,
                       final):
    n = pl.program_id(0)
    tm, d = x_ref.shape[1], x_ref.shape[2]
    f = act.shape[1]
    up_groups = _deal(list(range(f // MXU_COLS)), CONV_LAYER_SPLITS["up"])
    down_groups = _deal(list(range(d // MXU_COLS)), CONV_LAYER_SPLITS["down"])
    head_groups = _deal(list(range(d // MXU_COLS)), CONV_LAYER_SPLITS["head"])
    units = [(j, i) for j in range(d // LANES) for i in range(tm // UNIT_ROWS)]
    _first_step_init(n, gbuf, vbuf)

    @pl.when((n == 0) | (lax.rem(n + tiles_per_seq - 1, tiles_per_seq) == 0))
    def _():
        _shift_zero_halo(ubuf, CONV_HALO)

    def head_norm(after):
        m = mod_ref[0]
        x = x_ref[0] if after is None else _after(x_ref[0], after)
        ha[...] = _rms_mod(x, gm_ref[...], m[0:1], m[1:2]).astype(jnp.bfloat16)

    def head_matmul(blocks, after):
        cols = _block_cols(blocks)
        gate_cols = slice(d + cols.start, d + cols.stop)
        nc = cols.stop - cols.start
        w = jnp.concatenate([w1_ref[:, cols], w1_ref[:, gate_cols]], axis=1)
        b = jnp.concatenate([b1_ref[:, cols], b1_ref[:, gate_cols]], axis=1)
        u = jnp.dot(_operand_after(ha, after), w, preferred_element_type=jnp.float32) + b
        gbuf[:, cols] = u[:, :nc] * jax.nn.sigmoid(u[:, nc:])
        return _anchor(u)

    _conv_layer_step(xlag_ref, modlag_ref, wdw_ref, bdw_ref, lng_ref, lnb_ref, w2_ref,
                     b2_ref, gf_ref, wg_ref, wu_ref, wd_ref, fg_ref, o_ref, hb, act, gbuf, ubuf,
                     ybuf, vbuf, head_norm, head_matmul, units, up_groups, down_groups,
                     head_groups, final)


def _conv_layer_step(xlag_ref, modlag_ref, wdw_ref, bdw_ref, lng_ref, lnb_ref, w2_ref, b2_ref,
                     gf_ref, wg_ref, wu_ref, wd_ref, fg_ref, o_ref, hb, act, gbuf, ubuf, ybuf,
                     vbuf, head_norm, head_matmul, units, up_groups, down_groups, head_groups,
                     final):
    tm = xlag_ref.shape[1]
    unit_groups = iter(_deal(units, len(up_groups) + len(down_groups) - 1))
    _shift_store(ubuf, CONV_HALO, gbuf[...])
    y = jnp.dot(vbuf[...], w2_ref[...], preferred_element_type=jnp.float32) + b2_ref[...]
    _mixer_residual(xlag_ref, modlag_ref[0], y, gf_ref, o_ref, hb)

    done, ready = _anchor(y), None
    for k, blocks in enumerate(up_groups):
        done_next = _ffn_up_blocks(hb, wg_ref, wu_ref, act, blocks, ready)
        ready = _conv_units(next(unit_groups), wdw_ref, bdw_ref, ubuf, ybuf, done)
        if k == 0:
            head_norm(ready)
        done = done_next
    for k, blocks in enumerate(down_groups):
        done_next = _ffn_down_blocks(act, modlag_ref[0], wd_ref, o_ref, blocks, ready)
        if k + 1 < len(down_groups):
            ready = _conv_units(next(unit_groups), wdw_ref, bdw_ref, ubuf, ybuf, done)
            done = done_next
    v = ybuf[...]
    mu = jnp.mean(v, axis=-1, keepdims=True)
    vc = v - mu
    var = jnp.mean(vc * vc, axis=-1, keepdims=True)
    v = _silu(vc * lax.rsqrt(var + EPS) * lng_ref[...] + lnb_ref[...])
    vbuf[...] = v.astype(jnp.bfloat16)
    s = sum(v[:, j * LANES:(j + 1) * LANES] for j in range(v.shape[1] // LANES))
    ready = sum(s[i * SUBLANES:(i + 1) * SUBLANES] for i in range(tm // SUBLANES))
    for blocks in head_groups:
        head_matmul(blocks, ready)
    _shift_carry_halo(ubuf, CONV_HALO, tm)
    if final:
        _final_norm(o_ref, fg_ref)


def _pool_units(chunks, seq_row0, hbuf, mixbuf, after):
    slabs_per_group = hbuf.shape[0] // len(POOL_WINDOWS)
    for i in chunks:
        row0 = i * UNIT_ROWS
        zero = jnp.tile(_zero_of(after), (UNIT_ROWS // SUBLANES, 1))
        t = seq_row0 + row0 + lax.broadcasted_iota(jnp.int32, (UNIT_ROWS, LANES), 0)
        for gi, win in enumerate(POOL_WINDOWS):
            cnt = jnp.minimum(t + 1, win).astype(jnp.float32)
            for sl in range(gi * slabs_per_group, (gi + 1) * slabs_per_group):
                h = hbuf[sl, _shift_rows(POOL_HALO + row0, UNIT_ROWS), :]
                wsum = h + zero
                for j in range(1, win):
                    wsum = wsum + hbuf[sl, _shift_rows(POOL_HALO + row0 - j, UNIT_ROWS), :]
                mixbuf[row0:row0 + UNIT_ROWS, sl * LANES:(sl + 1) * LANES] = (
                    wsum / cnt - h).astype(jnp.bfloat16)
        after = _anchor(wsum)
    return after


def _pool_layer_kernel(x_ref, xlag_ref, mod_ref, modlag_ref, gm_ref, pw_ref, ls_ref, gf_ref,
                       wg_ref, wu_ref, wd_ref, fg_ref, o_ref, hb, act, hplain, hbuf, mixbuf, *,
                       tiles_per_seq, final):
    n = pl.program_id(0)
    tm, d = x_ref.shape[1], x_ref.shape[2]
    f = act.shape[1]
    window_tile = lax.rem(n + tiles_per_seq - 1, tiles_per_seq)
    up_groups = _deal(list(range(f // MXU_COLS)), POOL_LAYER_SPLITS["up"])
    down_groups = _deal(list(range(d // MXU_COLS)), POOL_LAYER_SPLITS["down"])
    chunks = list(range(tm // UNIT_ROWS))
    _first_step_init(n, hplain)

    @pl.when((n == 0) | (window_tile == 0))
    def _():
        _shift_zero_halo(hbuf, POOL_HALO)

    def head_norm(after):
        m = mod_ref[0]
        h = _rms_mod(_after(x_ref[0], after), gm_ref[...], m[0:1], m[1:2])
        hplain[...] = h
        return _anchor(h)

    @pl.when(n < LAG)
    def _():
        _shift_store(hbuf, POOL_HALO, hplain[...])
        done = _pool_units(chunks, window_tile * tm, hbuf, mixbuf, _anchor(x_ref[0]))
        _shift_carry_halo(hbuf, POOL_HALO, tm)
        head_norm(done)

    pl.when(n >= LAG)(functools.partial(
        _pool_layer_step, xlag_ref, modlag_ref, pw_ref, ls_ref, gf_ref, wg_ref, wu_ref, wd_ref,
        fg_ref, o_ref, hb, act, hplain, hbuf, mixbuf, head_norm, chunks, up_groups, down_groups,
        window_tile * tm, final))


def _pool_layer_step(xlag_ref, modlag_ref, pw_ref, ls_ref, gf_ref, wg_ref, wu_ref, wd_ref, fg_ref,
                     o_ref, hb, act, hplain, hbuf, mixbuf, head_norm, chunks, up_groups,
                     down_groups, seq_row0, final):
    tm, d = xlag_ref.shape[1], xlag_ref.shape[2]
    dg = d // len(POOL_WINDOWS)
    chunk_groups = _deal(chunks, len(up_groups) + len(down_groups) - 1)
    _shift_store(hbuf, POOL_HALO, hplain[...])
    ys = [jnp.dot(mixbuf[:, gi * dg:(gi + 1) * dg], pw_ref[gi],
                  preferred_element_type=jnp.float32) for gi in range(len(POOL_WINDOWS))]
    y = jnp.concatenate(ys, axis=1) * ls_ref[...]
    _mixer_residual(xlag_ref, modlag_ref[0], y, gf_ref, o_ref, hb)

    done, ready = _anchor(y), None
    matmuls = ([functools.partial(_ffn_up_blocks, hb, wg_ref, wu_ref, act, blocks)
                for blocks in up_groups]
               + [functools.partial(_ffn_down_blocks, act, modlag_ref[0], wd_ref, o_ref, blocks)
                  for blocks in down_groups])
    for k, matmul in enumerate(matmuls):
        done_next = matmul(ready)
        if k < len(chunk_groups):
            ready = _pool_units(chunk_groups[k], seq_row0, hbuf, mixbuf, done)
            done = done_next
        if k == len(chunk_groups) - 1:
            ready = head_norm(ready)
    _shift_carry_halo(hbuf, POOL_HALO, tm)
    if final:
        _final_norm(o_ref, fg_ref)


def _layer_call(body, name, x, mod, consts, scratch, final):
    bsz, seq, d = x.shape
    tm = SEQ_TILE
    tiles_per_seq = seq // tm
    n_tiles = bsz * tiles_per_seq
    head_tile = lambda n: jnp.minimum(n, n_tiles - 1)
    tail_tile = lambda n: jnp.maximum(n - LAG, 0)
    xt = x.reshape(n_tiles, tm, d)
    out = pl.pallas_call(
        functools.partial(body, tiles_per_seq=tiles_per_seq, final=final),
        grid=(n_tiles + LAG,),
        in_specs=[
            pl.BlockSpec((1, tm, d), lambda n: (head_tile(n), 0, 0)),
            pl.BlockSpec((1, tm, d), lambda n: (tail_tile(n), 0, 0)),
            pl.BlockSpec((1, N_ADA, d), lambda n: (head_tile(n) // tiles_per_seq, 0, 0)),
            pl.BlockSpec((1, N_ADA, d), lambda n: (tail_tile(n) // tiles_per_seq, 0, 0)),
        ] + [_const_spec(a, layer) for a, layer in consts],
        out_specs=pl.BlockSpec((1, tm, d), lambda n: (tail_tile(n), 0, 0)),
        out_shape=jax.ShapeDtypeStruct((n_tiles, tm, d), x.dtype),
        scratch_shapes=list(scratch),
        compiler_params=pltpu.CompilerParams(
            dimension_semantics=("arbitrary",), vmem_limit_bytes=VMEM_LIMIT),
        name=name,
    )(xt, xt, mod, mod, *[a for a, _ in consts])
    return out.reshape(bsz, seq, d)


def kernel(x, c, ada_w, ada_b, norm_mix_g, norm_ffn_g, conv_w1, conv_b1, conv_wdw, conv_bdw, conv_ln_g, conv_ln_b, conv_w2, conv_b2, pool_w, pool_ls, ffn_w_gate, ffn_w_up, ffn_w_down, final_g):
    depth = ada_w.shape[0]
    d = x.shape[-1]
    f = ffn_w_gate.shape[-1]
    whole = lambda a: (a.reshape(1, -1), None)
    bf16 = lambda a: a.astype(jnp.bfloat16)
    wg, wu, wd = bf16(ffn_w_gate), bf16(ffn_w_up), bf16(ffn_w_down)
    w1, w2, pw = bf16(conv_w1), bf16(conv_w2), bf16(pool_w)
    mod = _ada_mod(c, ada_w, ada_b)
    ffn_scratch = (pltpu.VMEM((SEQ_TILE, d), jnp.bfloat16), pltpu.VMEM((SEQ_TILE, f), jnp.bfloat16))
    for i in range(depth):
        j = i // 2
        final = i == depth - 1
        ffn_consts = (whole(norm_ffn_g[i]), (wg, i), (wu, i), (wd, i), whole(final_g))
        if i % 2 == 0:
            consts = (whole(norm_mix_g[i]), (w1, j), whole(conv_b1[j]), (conv_wdw, j),
                      whole(conv_bdw[j]), whole(conv_ln_g[j]), whole(conv_ln_b[j]),
                      (w2, j), whole(conv_b2[j])) + ffn_consts
            scratch = ffn_scratch + (pltpu.VMEM((SEQ_TILE, d), jnp.bfloat16),
                                     pltpu.VMEM((SEQ_TILE, d), jnp.float32),
                                     _shift_buffer(CONV_HALO + SEQ_TILE, d),
                                     pltpu.VMEM((SEQ_TILE, d), jnp.float32),
                                     pltpu.VMEM((SEQ_TILE, d), jnp.bfloat16))
            x = _layer_call(_conv_layer_kernel, "conv_layer", x, mod[i], consts, scratch, final)
        else:
            consts = (whole(norm_mix_g[i]), (pw, j), whole(pool_ls[j])) + ffn_consts
            scratch = ffn_scratch + (pltpu.VMEM((SEQ_TILE, d), jnp.float32),
                                     _shift_buffer(POOL_HALO + SEQ_TILE, d),
                                     pltpu.VMEM((SEQ_TILE, d), jnp.bfloat16))
            x = _layer_call(_pool_layer_kernel, "pool_layer", x, mod[i], consts, scratch, final)
    return x
```
